```python
import math
import jax
import jax.numpy as jnp
from jax import lax
import numpy as np

D_MODEL = 1024
BATCH = 16
SEQ = 2048
DEPTH = 1

MEM_TOKENS = 256
HYENA_WIDTH = 512
HYENA_ORDER = 2
FILTER_EMB = 33
FILTER_HIDDEN = 64
FILTER_OUT_SCALE = 0.1
HYENA_TARGET = 1e-2
HYENA_FAST_DECAY_PCT = 0.3
HYENA_SLOW_DECAY_PCT = 1.5
HYENA_MIN_DECAY = math.log(HYENA_TARGET) / HYENA_FAST_DECAY_PCT
HYENA_MAX_DECAY = math.log(HYENA_TARGET) / HYENA_SLOW_DECAY_PCT
DIFF_HEADS = 4
DIFF_HEAD_DIM = 64
DIFF_WIDTH = DIFF_HEADS * 2 * DIFF_HEAD_DIM
MEM_HEADS = 4
MEM_HEAD_DIM = 128
MEM_WIDTH = MEM_HEADS * MEM_HEAD_DIM
MIX_WIDTH = HYENA_WIDTH + DIFF_WIDTH + MEM_WIDTH
IN_WIDTH = 3 * HYENA_WIDTH + 3 * DIFF_WIDTH + MEM_WIDTH
D_FF = 2816
SHORT_CONV = 3
ROPE_THETA = 10000.0
Q_BLOCK = 128
LN_EPS = 1e-5
RMS_EPS = 1e-5
DEEPNORM_ALPHA = (2.0 * DEPTH) ** 0.25
DEEPNORM_BETA = (8.0 * DEPTH) ** -0.25

kernel_name = 'hybrid_hyena_diffattn_memory_encoder'


def layer_norm(x, g, b):
    xf = x.astype(jnp.float32)
    mu = jnp.mean(xf, axis=-1, keepdims=True)
    xc = xf - mu
    var = jnp.mean(xc * xc, axis=-1, keepdims=True)
    return (xc * lax.rsqrt(var + LN_EPS) * g.astype(jnp.float32) + b.astype(jnp.float32)).astype(x.dtype)


def dwconv3(x, w, b):
    xp = jnp.pad(x, ((0, 0), (1, 1), (0, 0)))
    return xp[:, :-2] * w[0] + xp[:, 1:-1] * w[1] + xp[:, 2:] * w[2] + b


def rope_tables(seq_len, dim):
    inv_freq = ROPE_THETA ** (-jnp.arange(0, dim, 2, dtype=jnp.float32) / dim)
    ang = jnp.arange(seq_len, dtype=jnp.float32)[:, None] * inv_freq[None, :]
    ang = jnp.concatenate([ang, ang], axis=-1)
    return jnp.cos(ang), jnp.sin(ang)


def apply_rope(t, cos, sin):
    half = t.shape[-1] // 2
    tf = t.astype(jnp.float32)
    rot = jnp.concatenate([-tf[..., half:], tf[..., :half]], axis=-1)
    return (tf * cos + rot * sin).astype(t.dtype)


def hyena_filter_spectrum(seq_len, w1, b1, freq, w2, b2, w3):
    f32 = jnp.float32
    t = jnp.linspace(0.0, 1.0, seq_len, dtype=f32)[:, None]
    bands = (FILTER_EMB - 1) // 2
    fr = jnp.linspace(1e-4, bands - 1, bands, dtype=f32)[None, :]
    w = 2.0 * math.pi * jnp.arange(seq_len, dtype=f32)[:, None] / seq_len
    z = jnp.concatenate([t, jnp.cos(fr * w), -jnp.sin(fr * w)], axis=-1)
    freq = freq.astype(f32)
    h = jnp.sin(freq * (z @ w1.astype(f32) + b1.astype(f32)))
    h = jnp.sin(freq * (h @ w2.astype(f32) + b2.astype(f32)))
    h = h @ w3.astype(f32)
    deltas = jnp.abs(jnp.linspace(HYENA_MIN_DECAY, HYENA_MAX_DECAY, HYENA_WIDTH, dtype=f32))
    decay = jnp.exp(-t * deltas[None, :])
    h = h.reshape(seq_len, HYENA_ORDER, 2, HYENA_WIDTH) * decay[:, None, None, :]
    h_fwd, h_bwd = h[:, :, 0], h[:, :, 1]
    k = jnp.concatenate([h_fwd, jnp.zeros_like(h_fwd[:1]), h_bwd[:0:-1]], axis=0)
    return jnp.fft.rfft(k, axis=0)


def fft_long_conv(z, k_f, bias):
    L = z.shape[1]
    z_f = jnp.fft.rfft(z, n=2 * L, axis=1)
    y = jnp.fft.irfft(z_f * k_f[None], n=2 * L, axis=1)[:, :L]
    return y + z * bias


def hyena_mixer(u, conv_w, conv_b, k_f, bias):
    u = dwconv3(u, conv_w, conv_b).astype(jnp.float32)
    v, x1, x2 = jnp.split(u, 3, axis=-1)
    bias = bias.astype(jnp.float32)
    z = x1 * fft_long_conv(v, k_f[:, 0], bias[0])
    z = x2 * fft_long_conv(z, k_f[:, 1], bias[1])
    return z


def diff_attention(q, k, v, lam_params, subln_g, lambda_init):
    B, S = q.shape[0], q.shape[1]
    d = DIFF_HEAD_DIM
    cos, sin = rope_tables(S, d)
    q = apply_rope(jnp.transpose(q, (0, 2, 3, 1, 4)), cos, sin)
    k = apply_rope(jnp.transpose(k, (0, 2, 3, 1, 4)), cos, sin)
    v = jnp.transpose(v, (0, 2, 1, 3))
    lp = lam_params.astype(jnp.float32)
    lam = jnp.exp(jnp.sum(lp[0] * lp[1])) - jnp.exp(jnp.sum(lp[2] * lp[3])) + lambda_init
    scale = d ** -0.5
    n_blk = S // Q_BLOCK
    q_blocks = jnp.moveaxis(q.reshape(B, DIFF_HEADS, 2, n_blk, Q_BLOCK, d), 3, 0)

    def attend(q_blk):
        s = jnp.einsum('bhcqd,bhckd->bhcqk', q_blk, k).astype(jnp.float32) * scale
        p = jax.nn.softmax(s, axis=-1)
        a = p[:, :, 0] - lam * p[:, :, 1]
        return jnp.einsum('bhqk,bhke->bhqe', a.astype(v.dtype), v)

    o = lax.map(attend, q_blocks)
    o = jnp.moveaxis(o, 0, 2).reshape(B, DIFF_HEADS, S, 2 * d)
    of = o.astype(jnp.float32)
    of = of * lax.rsqrt(jnp.mean(of * of, axis=-1, keepdims=True) + RMS_EPS)
    of = of * subln_g.astype(jnp.float32) * (1.0 - lambda_init)
    return jnp.transpose(of, (0, 2, 1, 3)).reshape(B, S, DIFF_WIDTH).astype(v.dtype)


def memory_attention(q, mem, w_kv):
    B, S = q.shape[0], q.shape[1]
    M = mem.shape[1]
    q = q.reshape(B, S, MEM_HEADS, MEM_HEAD_DIM)
    kv = mem @ w_kv
    k, v = jnp.split(kv, 2, axis=-1)
    k = k.reshape(B, M, MEM_HEADS, MEM_HEAD_DIM)
    v = v.reshape(B, M, MEM_HEADS, MEM_HEAD_DIM)
    s = jnp.einsum('bshd,bmhd->bhsm', q, k).astype(jnp.float32) * (MEM_HEAD_DIM ** -0.5)
    p = jax.nn.softmax(s, axis=-1)
    o = jnp.einsum('bhsm,bmhd->bshd', p.astype(v.dtype), v)
    return o.reshape(B, S, MEM_WIDTH)


def setup_inputs(seed: int = 0) -> dict:
    key = jax.random.key(seed)
    ks = jax.random.split(key, 24)
    f32 = jnp.float32

    def nrm(k, shape, scale):
        return jax.random.normal(k, shape, f32) * scale

    L = DEPTH
    return {
        'x': nrm(ks[0], (BATCH, SEQ, D_MODEL), 1.0),
        'mem': nrm(ks[1], (BATCH, MEM_TOKENS, D_MODEL), 1.0),
        'w_in': nrm(ks[2], (L, D_MODEL, IN_WIDTH), D_MODEL ** -0.5),
        'hy_conv_w': nrm(ks[3], (L, SHORT_CONV, 3 * HYENA_WIDTH), SHORT_CONV ** -0.5),
        'hy_conv_b': nrm(ks[4], (L, 3 * HYENA_WIDTH), 0.02),
        'hy_w1': nrm(ks[5], (L, FILTER_EMB, FILTER_HIDDEN), FILTER_EMB ** -0.5),
        'hy_b1': nrm(ks[6], (L, FILTER_HIDDEN), 0.1),
        'hy_freq': 1.0 + nrm(ks[7], (L, FILTER_HIDDEN), 0.1),
        'hy_w2': nrm(ks[8], (L, FILTER_HIDDEN, FILTER_HIDDEN), FILTER_HIDDEN ** -0.5),
        'hy_b2': nrm(ks[9], (L, FILTER_HIDDEN), 0.1),
        'hy_w3': nrm(ks[10], (L, FILTER_HIDDEN, 2 * HYENA_ORDER * HYENA_WIDTH), FILTER_OUT_SCALE * FILTER_HIDDEN ** -0.5),
        'hy_bias': nrm(ks[11], (L, HYENA_ORDER, HYENA_WIDTH), 0.5),
        'diff_lambda': nrm(ks[12], (L, 4, DIFF_HEAD_DIM), 0.1),
        'diff_subln_g': 1.0 + nrm(ks[13], (L, 2 * DIFF_HEAD_DIM), 0.02),
        'mem_w_kv': nrm(ks[14], (L, D_MODEL, 2 * MEM_WIDTH), D_MODEL ** -0.5),
        'w_out': nrm(ks[15], (L, MIX_WIDTH, D_MODEL), DEEPNORM_BETA * MIX_WIDTH ** -0.5),
        'ln1_g': 1.0 + nrm(ks[16], (L, D_MODEL), 0.02),
        'ln1_b': nrm(ks[17], (L, D_MODEL), 0.02),
        'ffn_w_up': nrm(ks[18], (L, D_MODEL, 2 * D_FF), D_MODEL ** -0.5),
        'ffn_conv_w': nrm(ks[19], (L, SHORT_CONV, 2 * D_FF), SHORT_CONV ** -0.5),
        'ffn_conv_b': nrm(ks[20], (L, 2 * D_FF), 0.02),
        'ffn_w_down': nrm(ks[21], (L, D_FF, D_MODEL), DEEPNORM_BETA * D_FF ** -0.5),
        'ln2_g': 1.0 + nrm(ks[22], (L, D_MODEL), 0.02),
        'ln2_b': nrm(ks[23], (L, D_MODEL), 0.02),
    }


def reference(x, mem, w_in, hy_conv_w, hy_conv_b, hy_w1, hy_b1, hy_freq, hy_w2, hy_b2, hy_w3,
              hy_bias, diff_lambda, diff_subln_g, mem_w_kv, w_out, ln1_g, ln1_b,
              ffn_w_up, ffn_conv_w, ffn_conv_b, ffn_w_down, ln2_g, ln2_b):
    B, S = x.shape[0], x.shape[1]
    s1 = 3 * HYENA_WIDTH
    s2 = s1 + DIFF_WIDTH
    s3 = s2 + DIFF_WIDTH
    s4 = s3 + DIFF_WIDTH
    for l in range(DEPTH):
        lambda_init = 0.8 - 0.6 * math.exp(-0.3 * l)
        proj = x @ w_in[l]
        hy_u, dq, dk, dv, mq = jnp.split(proj, [s1, s2, s3, s4], axis=-1)
        k_f = hyena_filter_spectrum(S, hy_w1[l], hy_b1[l], hy_freq[l], hy_w2[l], hy_b2[l], hy_w3[l])
        y_h = hyena_mixer(hy_u, hy_conv_w[l], hy_conv_b[l], k_f, hy_bias[l]).astype(x.dtype)
        y_d = diff_attention(dq.reshape(B, S, DIFF_HEADS, 2, DIFF_HEAD_DIM),
                             dk.reshape(B, S, DIFF_HEADS, 2, DIFF_HEAD_DIM),
                             dv.reshape(B, S, DIFF_HEADS, 2 * DIFF_HEAD_DIM),
                             diff_lambda[l], diff_subln_g[l], lambda_init)
        y_m = memory_attention(mq, mem, mem_w_kv[l])
        mix = jnp.concatenate([y_h, y_d, y_m], axis=-1) @ w_out[l]
        x = layer_norm(DEEPNORM_ALPHA * x + mix, ln1_g[l], ln1_b[l])
        h = dwconv3(x @ ffn_w_up[l], ffn_conv_w[l], ffn_conv_b[l])
        g, u = jnp.split(h, 2, axis=-1)
        y = (jax.nn.silu(g) * u) @ ffn_w_down[l]
        x = layer_norm(DEEPNORM_ALPHA * x + y, ln2_g[l], ln2_b[l])
    return x
```

```python
import functools
import math

import jax
import jax.numpy as jnp
import numpy as np
from jax import lax
from jax.experimental import pallas as pl
from jax.experimental.pallas import tpu as pltpu

F32 = jnp.float32
BF16 = jnp.bfloat16

HYENA_WIDTH = 512
HYENA_ORDER = 2
FILTER_EMB = 33
FILTER_HIDDEN = 64
HYENA_TARGET = 1e-2
HYENA_MIN_DECAY = math.log(HYENA_TARGET) / 0.3
HYENA_MAX_DECAY = math.log(HYENA_TARGET) / 1.5
DIFF_HEADS = 4
DIFF_HEAD_DIM = 64
DIFF_WIDTH = DIFF_HEADS * 2 * DIFF_HEAD_DIM
MEM_HEADS = 4
MEM_HEAD_DIM = 128
MEM_WIDTH = MEM_HEADS * MEM_HEAD_DIM
D_FF = 2816
ROPE_THETA = 10000.0
LN_EPS = 1e-5
RMS_EPS = 1e-5
DEPTH = 1
DEEPNORM_ALPHA = (2.0 * DEPTH) ** 0.25

LANES = 128
SUBLANES = 8
VMEM_LIMIT_BYTES = 56 * 1024 * 1024

PROJ_TM = 1024
PROJ_TN = 512
HY_CT = 256
DFT_TK = 256
DA_TQ = 256
MA_TS = 512
OUT_TM = 512
FFN_TM = 512
FFN_CW = 256
FFN_HALO = SUBLANES


def _cparams(sem):
    return pltpu.CompilerParams(dimension_semantics=sem, vmem_limit_bytes=VMEM_LIMIT_BYTES)


def _resident(block_shape, index_map):
    return pl.BlockSpec(block_shape, index_map, pipeline_mode=pl.Buffered(1))


@functools.lru_cache(maxsize=None)
def _dft_factor_tables(L):
    k = np.arange(L, dtype=np.int64)[:, None]
    a = np.arange(L // LANES, dtype=np.int64)[None, :]
    b = np.arange(LANES, dtype=np.int64)[None, :]
    ang_a = np.pi * ((k * a * LANES) % (2 * L)).astype(np.float64) / L
    ang_b = np.pi * ((k * b) % (2 * L)).astype(np.float64) / L
    f = lambda v: np.asarray(v, dtype=np.float32)
    return f(np.cos(ang_a)), f(np.sin(ang_a)), f(np.cos(ang_b)), f(np.sin(ang_b))


@functools.lru_cache(maxsize=None)
def _filter_features(L):
    t = np.linspace(0.0, 1.0, L, dtype=np.float64)[:, None]
    bands = (FILTER_EMB - 1) // 2
    fr = np.linspace(1e-4, bands - 1, bands, dtype=np.float64)[None, :]
    w = 2.0 * np.pi * np.arange(L, dtype=np.float64)[:, None] / L
    z = np.concatenate([t, np.cos(fr * w), -np.sin(fr * w)], axis=-1)
    zp = np.zeros((L, LANES), np.float64)
    zp[:, :FILTER_EMB] = z
    deltas = np.abs(np.linspace(HYENA_MIN_DECAY, HYENA_MAX_DECAY, HYENA_WIDTH, dtype=np.float64))[None, :]
    return np.asarray(zp, np.float32), np.asarray(t, np.float32), np.asarray(deltas, np.float32)


@functools.lru_cache(maxsize=None)
def _rope_tables(S):
    d = DIFF_HEAD_DIM
    inv_freq = ROPE_THETA ** (-np.arange(0, d, 2, dtype=np.float64) / d)
    ang = np.arange(S, dtype=np.float64)[:, None] * inv_freq[None, :]
    ang = np.concatenate([ang, ang, ang, ang], axis=-1)
    return np.asarray(np.cos(ang), np.float32), np.asarray(np.sin(ang), np.float32)


def _split_bf16(a):
    hi = a.astype(BF16)
    lo = (a - hi.astype(F32)).astype(BF16)
    return hi, lo


def _dot(a, b):
    return jnp.dot(a, b, preferred_element_type=F32)


def _dot_split(a, b):
    ah, al = _split_bf16(a)
    bh, bl = _split_bf16(b)
    return _dot(ah, bh) + _dot(al, bh) + _dot(ah, bl)


def _shift_rows(u):
    n = u.shape[0]
    row = lax.broadcasted_iota(jnp.int32, u.shape, 0)
    prev = jnp.where(row == 0, 0.0, pltpu.roll(u, 1, 0))
    nxt = jnp.where(row == n - 1, 0.0, pltpu.roll(u, n - 1, 0))
    return prev, nxt


def _layer_norm(r, g, b):
    mu = jnp.mean(r, axis=-1, keepdims=True)
    rc = r - mu
    var = jnp.mean(rc * rc, axis=-1, keepdims=True)
    return rc * lax.rsqrt(var + LN_EPS) * g + b


def _proj_kernel(x_ref, w_ref, o_ref, xb_ref):
    @pl.when(pl.program_id(1) == 0)
    def _():
        xb_ref[...] = x_ref[...].astype(BF16)

    o_ref[...] = _dot(xb_ref[...], w_ref[...]).astype(o_ref.dtype)


def _proj(x2d, w_bf16, name):
    M, K = x2d.shape
    N = w_bf16.shape[1]
    tm, tn = min(PROJ_TM, M), min(PROJ_TN, N)
    return pl.pallas_call(
        _proj_kernel,
        out_shape=jax.ShapeDtypeStruct((M, N), BF16),
        grid=(M // tm, N // tn),
        in_specs=[pl.BlockSpec((tm, K), lambda i, j: (i, 0)),
                  pl.BlockSpec((K, tn), lambda i, j: (0, j))],
        out_specs=pl.BlockSpec((tm, tn), lambda i, j: (i, j)),
        scratch_shapes=[pltpu.VMEM((tm, K), BF16)],
        compiler_params=_cparams(("parallel", "arbitrary")),
        name=name,
    )(x2d, w_bf16)


def _filter_kernel(z_ref, t_ref, dl_ref, w1_ref, b1_ref, fq_ref, w2_ref, b2_ref, w3_ref,
                   eh_ref, el_ref, dh_ref, dlo_ref, kn_ref, *, L):
    tl = z_ref.shape[0]
    C = HYENA_WIDTH
    fq = fq_ref[...]
    h = jnp.sin(fq * (_dot_split(z_ref[...], w1_ref[...]) + b1_ref[...]))
    h = jnp.sin(fq * (_dot_split(h, w2_ref[...]) + b2_ref[...]))
    decay = jnp.exp(-t_ref[...] * dl_ref[...])
    row = pl.program_id(0) * tl + lax.broadcasted_iota(jnp.int32, (tl, C), 0)
    sign = jnp.where(row % 2 == 0, 1.0, -1.0)

    @pl.when(pl.program_id(0) == 0)
    def _():
        kn_ref[...] = jnp.zeros_like(kn_ref)

    for o in range(HYENA_ORDER):
        ho = _dot_split(h, w3_ref[:, o * 2 * C:(o + 1) * 2 * C])
        fwd = ho[:, :C] * decay
        bwd = ho[:, C:] * decay
        e = fwd + jnp.where(row == 0, 0.0, bwd)
        d = fwd - bwd
        eh, el = _split_bf16(e)
        dh, dl = _split_bf16(d)
        eh_ref[:, o * C:(o + 1) * C] = eh
        el_ref[:, o * C:(o + 1) * C] = el
        dh_ref[:, o * C:(o + 1) * C] = dh
        dlo_ref[:, o * C:(o + 1) * C] = dl
        kn_ref[:, o * C:(o + 1) * C] += jnp.sum(e * sign, axis=0, keepdims=True) * (1.0 / (2 * L))


def _filter_time(L, w1, b1, freq, w2, b2, w3):
    zp, t, deltas = _filter_features(L)
    H = LANES
    pad_c = H - FILTER_HIDDEN
    w1p = jnp.pad(w1.astype(F32), ((0, LANES - FILTER_EMB), (0, pad_c)))
    w2p = jnp.pad(w2.astype(F32), ((0, pad_c), (0, pad_c)))
    w3p = jnp.pad(w3.astype(F32), ((0, pad_c), (0, 0)))
    row = lambda v: jnp.pad(v.astype(F32), (0, pad_c)).reshape(1, H)
    OC = HYENA_ORDER * HYENA_WIDTH
    outs = [jax.ShapeDtypeStruct((L, OC), BF16)] * 4 + [jax.ShapeDtypeStruct((1, OC), F32)]
    tl = DFT_TK
    rows = lambda w: pl.BlockSpec((tl, w), lambda i: (i, 0))
    whole = lambda r, c: pl.BlockSpec((r, c), lambda i: (0, 0))
    return pl.pallas_call(
        functools.partial(_filter_kernel, L=L),
        out_shape=outs,
        grid=(L // tl,),
        in_specs=[rows(H), rows(1), whole(1, HYENA_WIDTH), whole(H, H), whole(1, H), whole(1, H),
                  whole(H, H), whole(1, H), whole(H, 2 * OC)],
        out_specs=[rows(OC)] * 4 + [whole(1, OC)],
        compiler_params=_cparams(("arbitrary",)),
        name="hyena_filter_time",
    )(jnp.asarray(zp), jnp.asarray(t), jnp.asarray(deltas), w1p, row(b1), row(freq), w2p, row(b2), w3p)


def _spectrum_kernel(ca_ref, sa_ref, cb_ref, sb_ref, eh_ref, el_ref, dh_ref, dl_ref,
                     c_ref, s_ref, q1_ref, q2_ref, cf_ref, sf_ref):
    tk, L = c_ref.shape
    cb, sb = cb_ref[...], sb_ref[...]
    for a in range(L // LANES):
        ca = ca_ref[:, a:a + 1]
        sa = sa_ref[:, a:a + 1]
        cf_ref[:, a * LANES:(a + 1) * LANES] = ca * cb - sa * sb
        sf_ref[:, a * LANES:(a + 1) * LANES] = sa * cb + ca * sb
    ch, cl = _split_bf16(cf_ref[...])
    sh, sl = _split_bf16(sf_ref[...])
    c_ref[...] = ch
    s_ref[...] = sh
    kc = _dot(ch, eh_ref[...]) + _dot(cl, eh_ref[...]) + _dot(ch, el_ref[...])
    ks = _dot(sh, dh_ref[...]) + _dot(sl, dh_ref[...]) + _dot(sh, dl_ref[...])
    k_glob = pl.program_id(0) * tk + lax.broadcasted_iota(jnp.int32, kc.shape, 0)
    scale = jnp.where(k_glob == 0, 0.5 / L, 1.0 / L)
    q1_ref[...] = kc * scale
    q2_ref[...] = ks * scale


def _spectrum(L, eh, el, dh, dl):
    ca, sa, cb, sb = _dft_factor_tables(L)
    OC = eh.shape[1]
    tk = DFT_TK
    na = L // LANES
    rows = lambda w: pl.BlockSpec((tk, w), lambda i: (i, 0))
    full = _resident((L, OC), lambda i: (0, 0))
    return pl.pallas_call(
        _spectrum_kernel,
        out_shape=[jax.ShapeDtypeStruct((L, L), BF16), jax.ShapeDtypeStruct((L, L), BF16),
                   jax.ShapeDtypeStruct((L, OC), F32), jax.ShapeDtypeStruct((L, OC), F32)],
        grid=(L // tk,),
        in_specs=[rows(na), rows(na), rows(LANES), rows(LANES), full, full, full, full],
        out_specs=[rows(L), rows(L), rows(OC), rows(OC)],
        scratch_shapes=[pltpu.VMEM((tk, L), F32), pltpu.VMEM((tk, L), F32)],
        compiler_params=_cparams(("parallel",)),
        name="hyena_filter_spectrum",
    )(jnp.asarray(ca), jnp.asarray(sa), jnp.asarray(cb), jnp.asarray(sb), eh, el, dh, dl)


def _hyena_kernel(v_ref, x1_ref, x2_ref, wv_ref, wx1_ref, wx2_ref, bv_ref, bx1_ref, bx2_ref,
                  c_ref, s_ref, q1a_ref, q2a_ref, q1b_ref, q2b_ref, kna_ref, knb_ref,
                  bias_a_ref, bias_b_ref, o_ref):
    L = v_ref.shape[1]

    def short_conv(u_ref, w_ref, b_ref):
        u = u_ref[0].astype(F32)
        prev, nxt = _shift_rows(u)
        return prev * w_ref[0:1, :] + u * w_ref[1:2, :] + nxt * w_ref[2:3, :] + b_ref[...]

    row = lax.broadcasted_iota(jnp.int32, (L, v_ref.shape[2]), 0)
    sign = jnp.where(row % 2 == 0, 1.0, -1.0)

    def long_conv(z, q1_ref, q2_ref, kn_ref, bias_ref):
        zb = z.astype(BF16)
        zc = _dot(c_ref[...], zb)
        zs = _dot(s_ref[...], zb)
        z_nyq = jnp.sum(z * sign, axis=0, keepdims=True)
        q1, q2 = q1_ref[...], q2_ref[...]
        ya = (zc * q1 - zs * q2).astype(BF16)
        yb = (zc * q2 + zs * q1).astype(BF16)
        y = _dot(c_ref[...], ya) + _dot(s_ref[...], yb)
        return y + sign * (z_nyq * kn_ref[...]) + z * bias_ref[...]

    v = short_conv(v_ref, wv_ref, bv_ref)
    x1 = short_conv(x1_ref, wx1_ref, bx1_ref)
    z = x1 * long_conv(v, q1a_ref, q2a_ref, kna_ref, bias_a_ref)
    x2 = short_conv(x2_ref, wx2_ref, bx2_ref)
    o_ref[0] = (x2 * long_conv(z, q1b_ref, q2b_ref, knb_ref, bias_b_ref)).astype(o_ref.dtype)


def _hyena(proj3, conv_w, conv_b, ctab, stab, q1, q2, knyq, bias):
    B, L, _ = proj3.shape
    C = HYENA_WIDTH
    ct = HY_CT
    nct = C // ct
    conv_w = conv_w.astype(F32)
    conv_b2 = conv_b.astype(F32).reshape(1, 3 * C)
    bias = bias.astype(F32)
    u_spec = lambda g: pl.BlockSpec((1, L, ct), lambda c, b, g=g: (b, 0, g * nct + c))
    w_spec = lambda g: pl.BlockSpec((3, ct), lambda c, b, g=g: (0, g * nct + c))
    b_spec = lambda g: pl.BlockSpec((1, ct), lambda c, b, g=g: (0, g * nct + c))
    tab = _resident((L, L), lambda c, b: (0, 0))
    q_spec = lambda o: _resident((L, ct), lambda c, b, o=o: (0, o * nct + c))
    r_spec = lambda o: pl.BlockSpec((1, ct), lambda c, b, o=o: (0, o * nct + c))
    bias2 = bias.reshape(1, HYENA_ORDER * C)
    return pl.pallas_call(
        _hyena_kernel,
        out_shape=jax.ShapeDtypeStruct((B, L, C), BF16),
        grid=(nct, B),
        in_specs=[u_spec(0), u_spec(1), u_spec(2), w_spec(0), w_spec(1), w_spec(2),
                  b_spec(0), b_spec(1), b_spec(2), tab, tab,
                  q_spec(0), q_spec(0), q_spec(1), q_spec(1), r_spec(0), r_spec(1), r_spec(0), r_spec(1)],
        out_specs=pl.BlockSpec((1, L, ct), lambda c, b: (b, 0, c)),
        compiler_params=_cparams(("parallel", "parallel")),
        name="hyena_mixer",
    )(proj3, proj3, proj3, conv_w, conv_w, conv_w, conv_b2, conv_b2, conv_b2, ctab, stab,
      q1, q2, q1, q2, knyq, knyq, bias2, bias2)


def _rope(t, cos, sin):
    lane = lax.broadcasted_iota(jnp.int32, t.shape, 1)
    first = (lane % DIFF_HEAD_DIM) < (DIFF_HEAD_DIM // 2)
    up = pltpu.roll(t, LANES - DIFF_HEAD_DIM // 2, 1)
    down = pltpu.roll(t, DIFF_HEAD_DIM // 2, 1)
    rot = jnp.where(first, -up, down)
    return t * cos + rot * sin


def _diff_attn_kernel(q_ref, k_ref, v_ref, cq_ref, sq_ref, ck_ref, sk_ref, lam_ref, g_ref,
                      o_ref, kr_ref, *, lambda_init):
    tq = q_ref.shape[1]

    @pl.when(pl.program_id(2) == 0)
    def _():
        kr_ref[...] = _rope(k_ref[0].astype(F32), ck_ref[...], sk_ref[...]).astype(BF16)

    scale = DIFF_HEAD_DIM ** -0.5
    q = (_rope(q_ref[0].astype(F32), cq_ref[...], sq_ref[...]) * scale).astype(BF16)
    lane = lax.broadcasted_iota(jnp.int32, q.shape, 1)
    zero = jnp.zeros_like(q)
    qq = jnp.concatenate([jnp.where(lane < DIFF_HEAD_DIM, q, zero),
                          jnp.where(lane >= DIFF_HEAD_DIM, q, zero)], axis=0)
    s = lax.dot_general(qq, kr_ref[...], (((1,), (1,)), ((), ())), preferred_element_type=F32)
    m = jnp.max(s, axis=-1, keepdims=True)
    p = jnp.exp(s - m)
    pn = p * (1.0 / jnp.sum(p, axis=-1, keepdims=True))
    lp = lam_ref[...]
    lam = (jnp.exp(jnp.sum(lp[0:1] * lp[1:2], axis=-1, keepdims=True))
           - jnp.exp(jnp.sum(lp[2:3] * lp[3:4], axis=-1, keepdims=True)) + lambda_init)
    a = (pn[:tq] - lam * pn[tq:]).astype(BF16)
    o = _dot(a, v_ref[0])
    o = o * lax.rsqrt(jnp.mean(o * o, axis=-1, keepdims=True) + RMS_EPS)
    o_ref[0] = (o * g_ref[...] * (1.0 - lambda_init)).astype(o_ref.dtype)


def _diff_attn(proj3, lam_params, subln_g, lambda_init, q_col, k_col, v_col):
    B, S, _ = proj3.shape
    hw = 2 * DIFF_HEAD_DIM
    tq = DA_TQ
    cos, sin = _rope_tables(S)
    cos, sin = jnp.asarray(cos), jnp.asarray(sin)
    qb, kb, vb = q_col // hw, k_col // hw, v_col // hw
    kernel = functools.partial(_diff_attn_kernel, lambda_init=lambda_init)
    return pl.pallas_call(
        kernel,
        out_shape=jax.ShapeDtypeStruct((B, S, DIFF_WIDTH), BF16),
        grid=(B, DIFF_HEADS, S // tq),
        in_specs=[pl.BlockSpec((1, tq, hw), lambda b, h, i: (b, i, qb + h)),
                  pl.BlockSpec((1, S, hw), lambda b, h, i: (b, 0, kb + h)),
                  pl.BlockSpec((1, S, hw), lambda b, h, i: (b, 0, vb + h)),
                  pl.BlockSpec((tq, hw), lambda b, h, i: (i, 0)),
                  pl.BlockSpec((tq, hw), lambda b, h, i: (i, 0)),
                  pl.BlockSpec((S, hw), lambda b, h, i: (0, 0)),
                  pl.BlockSpec((S, hw), lambda b, h, i: (0, 0)),
                  pl.BlockSpec((4, DIFF_HEAD_DIM), lambda b, h, i: (0, 0)),
                  pl.BlockSpec((1, hw), lambda b, h, i: (0, 0))],
        out_specs=pl.BlockSpec((1, tq, hw), lambda b, h, i: (b, i, h)),
        scratch_shapes=[pltpu.VMEM((S, hw), BF16)],
        compiler_params=_cparams(("parallel", "parallel", "arbitrary")),
        name="diff_attention",
    )(proj3, proj3, proj3, cos, sin, cos, sin, lam_params.astype(F32),
      subln_g.astype(F32).reshape(1, hw))


def _mem_attn_kernel(q_ref, kv_ref, o_ref):
    scale = MEM_HEAD_DIM ** -0.5
    d = MEM_HEAD_DIM
    for h in range(MEM_HEADS):
        q = q_ref[0, :, h * d:(h + 1) * d]
        k = kv_ref[0, :, h * d:(h + 1) * d]
        v = kv_ref[0, :, MEM_WIDTH + h * d:MEM_WIDTH + (h + 1) * d]
        s = lax.dot_general(q, k, (((1,), (1,)), ((), ())), preferred_element_type=F32) * scale
        m = jnp.max(s, axis=-1, keepdims=True)
        p = jnp.exp(s - m)
        pn = p * (1.0 / jnp.sum(p, axis=-1, keepdims=True))
        o_ref[0, :, h * d:(h + 1) * d] = _dot(pn.astype(BF16), v).astype(o_ref.dtype)


def _mem_attn(proj3, kv3, q_col):
    B, S, _ = proj3.shape
    M = kv3.shape[1]
    ts = MA_TS
    qb = q_col // MEM_WIDTH
    return pl.pallas_call(
        _mem_attn_kernel,
        out_shape=jax.ShapeDtypeStruct((B, S, MEM_WIDTH), BF16),
        grid=(B, S // ts),
        in_specs=[pl.BlockSpec((1, ts, MEM_WIDTH), lambda b, i: (b, i, qb)),
                  pl.BlockSpec((1, M, 2 * MEM_WIDTH), lambda b, i: (b, 0, 0))],
        out_specs=pl.BlockSpec((1, ts, MEM_WIDTH), lambda b, i: (b, i, 0)),
        compiler_params=_cparams(("parallel", "parallel")),
        name="memory_attention",
    )(proj3, kv3)


def _out_ln_kernel(yh_ref, yd_ref, ym_ref, x_ref, w_ref, g_ref, b_ref, o_ref):
    wh = yh_ref.shape[1]
    wd = yd_ref.shape[1]
    mix = (_dot(yh_ref[...], w_ref[0:wh, :]) + _dot(yd_ref[...], w_ref[wh:wh + wd, :])
           + _dot(ym_ref[...], w_ref[wh + wd:, :]))
    o_ref[...] = _layer_norm(DEEPNORM_ALPHA * x_ref[...] + mix, g_ref[...], b_ref[...])


def _out_ln(yh, yd, ym, x2d, w_out_bf16, g, b):
    M, D = x2d.shape
    tm = OUT_TM
    rows = lambda w: pl.BlockSpec((tm, w), lambda i: (i, 0))
    vec = pl.BlockSpec((1, D), lambda i: (0, 0))
    return pl.pallas_call(
        _out_ln_kernel,
        out_shape=jax.ShapeDtypeStruct((M, D), F32),
        grid=(M // tm,),
        in_specs=[rows(yh.shape[1]), rows(yd.shape[1]), rows(ym.shape[1]), rows(D),
                  _resident(w_out_bf16.shape, lambda i: (0, 0)), vec, vec],
        out_specs=rows(D),
        compiler_params=_cparams(("parallel",)),
        name="out_proj_ln",
    )(yh, yd, ym, x2d, w_out_bf16, g.astype(F32).reshape(1, D), b.astype(F32).reshape(1, D))


def _ffn_kernel(xm_ref, xp_ref, xn_ref, wup_ref, cw_ref, cb_ref, wdn_ref, g_ref, b_ref, o_ref):
    tm = xm_ref.shape[1]
    halo = xp_ref.shape[1]
    nchunk = wdn_ref.shape[0]
    i = pl.program_id(1)
    last = pl.num_programs(1) - 1
    xm = xm_ref[0]
    xp = jnp.where(i == 0, 0.0, xp_ref[0])
    xn = jnp.where(i == last, 0.0, xn_ref[0])
    xe = jnp.concatenate([xp, xm, xn], axis=0).astype(BF16)

    def conv(hc, j):
        n = hc.shape[0]
        prev = pltpu.roll(hc, 1, 0)[halo:halo + tm]
        nxt = pltpu.roll(hc, n - 1, 0)[halo:halo + tm]
        w = cw_ref[j]
        return prev * w[0:1] + hc[halo:halo + tm] * w[1:2] + nxt * w[2:3] + cb_ref[j]

    def body(j, acc):
        hg = conv(_dot(xe, wup_ref[j]), j)
        hu = conv(_dot(xe, wup_ref[nchunk + j]), nchunk + j)
        act = hg * (1.0 / (1.0 + jnp.exp(-hg))) * hu
        return acc + _dot(act.astype(BF16), wdn_ref[j])

    y = lax.fori_loop(0, nchunk, body, jnp.zeros(xm.shape, F32))
    o_ref[0] = _layer_norm(DEEPNORM_ALPHA * xm + y, g_ref[...], b_ref[...])


def _ffn(x3, w_up, conv_w, conv_b, w_down, g, b):
    B, S, D = x3.shape
    tm, cw, halo = FFN_TM, FFN_CW, FFN_HALO
    nchunk = D_FF // cw
    wup = w_up.astype(BF16).reshape(D, 2 * nchunk, cw).transpose(1, 0, 2)
    cwt = conv_w.astype(F32).reshape(3, 2 * nchunk, cw).transpose(1, 0, 2)
    cbt = conv_b.astype(F32).reshape(2 * nchunk, 1, cw)
    wdn = w_down.astype(BF16).reshape(nchunk, cw, D)
    r = tm // halo
    nblk = S // halo
    vec = pl.BlockSpec((1, D), lambda b_, i: (0, 0))
    return pl.pallas_call(
        _ffn_kernel,
        out_shape=jax.ShapeDtypeStruct((B, S, D), F32),
        grid=(B, S // tm),
        in_specs=[pl.BlockSpec((1, tm, D), lambda b_, i: (b_, i, 0)),
                  pl.BlockSpec((1, halo, D), lambda b_, i: (b_, jnp.maximum(i * r - 1, 0), 0)),
                  pl.BlockSpec((1, halo, D), lambda b_, i: (b_, jnp.minimum((i + 1) * r, nblk - 1), 0)),
                  _resident(wup.shape, lambda b_, i: (0, 0, 0)),
                  _resident(cwt.shape, lambda b_, i: (0, 0, 0)),
                  _resident(cbt.shape, lambda b_, i: (0, 0, 0)),
                  _resident(wdn.shape, lambda b_, i: (0, 0, 0)),
                  vec, vec],
        out_specs=pl.BlockSpec((1, tm, D), lambda b_, i: (b_, i, 0)),
        compiler_params=_cparams(("parallel", "parallel")),
        name="conv_gated_mlp_ln",
    )(x3, x3, x3, wup, cwt, cbt, wdn, g.astype(F32).reshape(1, D), b.astype(F32).reshape(1, D))


def kernel(x, mem, w_in, hy_conv_w, hy_conv_b, hy_w1, hy_b1, hy_freq, hy_w2, hy_b2, hy_w3, hy_bias,
           diff_lambda, diff_subln_g, mem_w_kv, w_out, ln1_g, ln1_b, ffn_w_up, ffn_conv_w, ffn_conv_b,
           ffn_w_down, ln2_g, ln2_b):
    B, S, D = x.shape
    M = mem.shape[1]
    s1 = 3 * HYENA_WIDTH
    q_col, k_col, v_col = s1, s1 + DIFF_WIDTH, s1 + 2 * DIFF_WIDTH
    mq_col = s1 + 3 * DIFF_WIDTH
    for l in range(DEPTH):
        lambda_init = 0.8 - 0.6 * math.exp(-0.3 * l)
        x2d = x.reshape(B * S, D)
        proj3 = _proj(x2d, w_in[l].astype(BF16), "in_proj").reshape(B, S, -1)
        kv3 = _proj(mem.reshape(B * M, D), mem_w_kv[l].astype(BF16), "mem_kv_proj").reshape(B, M, -1)

        eh, el, dh, dl, knyq = _filter_time(S, hy_w1[l], hy_b1[l], hy_freq[l], hy_w2[l], hy_b2[l], hy_w3[l])
        ctab, stab, q1, q2 = _spectrum(S, eh, el, dh, dl)
        y_h = _hyena(proj3, hy_conv_w[l], hy_conv_b[l], ctab, stab, q1, q2, knyq, hy_bias[l])
        y_d = _diff_attn(proj3, diff_lambda[l], diff_subln_g[l], lambda_init, q_col, k_col, v_col)
        y_m = _mem_attn(proj3, kv3, mq_col)

        x1 = _out_ln(y_h.reshape(B * S, -1), y_d.reshape(B * S, -1), y_m.reshape(B * S, -1), x2d,
                     w_out[l].astype(BF16), ln1_g[l], ln1_b[l])
        x = _ffn(x1.reshape(B, S, D), ffn_w_up[l], ffn_conv_w[l], ffn_conv_b[l], ffn_w_down[l],
                 ln2_g[l], ln2_b[l])
    return x
```

```python
import functools
import math

import jax
import jax.numpy as jnp
import numpy as np
from jax import lax
from jax.experimental import pallas as pl
from jax.experimental.pallas import tpu as pltpu

F32 = jnp.float32
BF16 = jnp.bfloat16

HYENA_WIDTH = 512
HYENA_ORDER = 2
FILTER_EMB = 33
FILTER_HIDDEN = 64
HYENA_TARGET = 1e-2
HYENA_MIN_DECAY = math.log(HYENA_TARGET) / 0.3
HYENA_MAX_DECAY = math.log(HYENA_TARGET) / 1.5
DIFF_HEADS = 4
DIFF_HEAD_DIM = 64
DIFF_WIDTH = DIFF_HEADS * 2 * DIFF_HEAD_DIM
MEM_HEADS = 4
MEM_HEAD_DIM = 128
MEM_WIDTH = MEM_HEADS * MEM_HEAD_DIM
D_FF = 2816
ROPE_THETA = 10000.0
LN_EPS = 1e-5
RMS_EPS = 1e-5
DEPTH = 1
DEEPNORM_ALPHA = (2.0 * DEPTH) ** 0.25
LOG2_E = 1.4426950408889634

LANES = 128
SUBLANES = 8
VMEM_LIMIT_BYTES = 56 * 1024 * 1024

PROJ_TM = 1024
PROJ_TN = 512
HY_CT = 256
HY_KB = 512
HY_MB = 512
DFT_TK = 256
DA_TQ = 256
MA_TS = 512
OUT_TM = 512
FFN_TM = 512
FFN_CW = 256
FFN_HALO = SUBLANES


def _cparams(sem):
    return pltpu.CompilerParams(dimension_semantics=sem, vmem_limit_bytes=VMEM_LIMIT_BYTES)


def _resident(block_shape, index_map):
    return pl.BlockSpec(block_shape, index_map, pipeline_mode=pl.Buffered(1))


@functools.lru_cache(maxsize=None)
def _dft_factor_tables(L):
    k = np.arange(L, dtype=np.int64)[:, None]
    a = np.arange(L // LANES, dtype=np.int64)[None, :]
    b = np.arange(LANES, dtype=np.int64)[None, :]
    ang_a = np.pi * ((k * a * LANES) % (2 * L)).astype(np.float64) / L
    ang_b = np.pi * ((k * b) % (2 * L)).astype(np.float64) / L
    f = lambda v: np.asarray(v, dtype=np.float32)
    return f(np.cos(ang_a)), f(np.sin(ang_a)), f(np.cos(ang_b)), f(np.sin(ang_b))


@functools.lru_cache(maxsize=None)
def _filter_features(L):
    t = np.linspace(0.0, 1.0, L, dtype=np.float64)[:, None]
    bands = (FILTER_EMB - 1) // 2
    fr = np.linspace(1e-4, bands - 1, bands, dtype=np.float64)[None, :]
    w = 2.0 * np.pi * np.arange(L, dtype=np.float64)[:, None] / L
    z = np.concatenate([t, np.cos(fr * w), -np.sin(fr * w)], axis=-1)
    zp = np.zeros((L, LANES), np.float64)
    zp[:, :FILTER_EMB] = z
    deltas = np.abs(np.linspace(HYENA_MIN_DECAY, HYENA_MAX_DECAY, HYENA_WIDTH, dtype=np.float64))[None, :]
    return np.asarray(zp, np.float32), np.asarray(t, np.float32), np.asarray(deltas, np.float32)


@functools.lru_cache(maxsize=None)
def _rope_tables(S):
    d = DIFF_HEAD_DIM
    inv_freq = ROPE_THETA ** (-np.arange(0, d, 2, dtype=np.float64) / d)
    ang = np.arange(S, dtype=np.float64)[:, None] * inv_freq[None, :]
    ang = np.concatenate([ang, ang, ang, ang], axis=-1)
    return np.asarray(np.cos(ang), np.float32), np.asarray(np.sin(ang), np.float32)


def _split_bf16(a):
    hi = a.astype(BF16)
    lo = (a - hi.astype(F32)).astype(BF16)
    return hi, lo


def _dot(a, b):
    return jnp.dot(a, b, preferred_element_type=F32)


def _dot_split(a, b):
    ah, al = _split_bf16(a)
    bh, bl = _split_bf16(b)
    return _dot(ah, bh) + _dot(al, bh) + _dot(ah, bl)


def _shift_rows(u):
    n = u.shape[0]
    row = lax.broadcasted_iota(jnp.int32, u.shape, 0)
    prev = jnp.where(row == 0, 0.0, pltpu.roll(u, 1, 0))
    nxt = jnp.where(row == n - 1, 0.0, pltpu.roll(u, n - 1, 0))
    return prev, nxt


def _layer_norm(r, g, b):
    mu = jnp.mean(r, axis=-1, keepdims=True)
    rc = r - mu
    var = jnp.mean(rc * rc, axis=-1, keepdims=True)
    return rc * lax.rsqrt(var + LN_EPS) * g + b


def _proj_kernel(x_ref, w_ref, o_ref, xb_ref):
    xb_ref[...] = x_ref[...].astype(BF16)
    n = w_ref.shape[1]
    tn = min(PROJ_TN, n)
    for j in range(n // tn):
        c = slice(j * tn, (j + 1) * tn)
        o_ref[:, c] = _dot(xb_ref[...], w_ref[:, c]).astype(o_ref.dtype)


def _proj(x2d, w_bf16, name):
    M, K = x2d.shape
    N = w_bf16.shape[1]
    tm = min(PROJ_TM, M)
    return pl.pallas_call(
        _proj_kernel,
        out_shape=jax.ShapeDtypeStruct((M, N), BF16),
        grid=(M // tm,),
        in_specs=[pl.BlockSpec((tm, K), lambda i: (i, 0)),
                  _resident((K, N), lambda i: (0, 0))],
        out_specs=pl.BlockSpec((tm, N), lambda i: (i, 0)),
        scratch_shapes=[pltpu.VMEM((tm, K), BF16)],
        compiler_params=_cparams(("parallel",)),
        name=name,
    )(x2d, w_bf16)


def _filter_kernel(z_ref, t_ref, dl_ref, w1_ref, b1_ref, fq_ref, w2_ref, b2_ref, w3_ref,
                   eh_ref, el_ref, dh_ref, dlo_ref, kn_ref, *, L):
    tl = z_ref.shape[0]
    C = HYENA_WIDTH
    fq = fq_ref[...]
    h = jnp.sin(fq * (_dot_split(z_ref[...], w1_ref[...]) + b1_ref[...]))
    h = jnp.sin(fq * (_dot_split(h, w2_ref[...]) + b2_ref[...]))
    decay = jnp.exp(-t_ref[...] * dl_ref[...])
    row = pl.program_id(0) * tl + lax.broadcasted_iota(jnp.int32, (tl, C), 0)
    sign = jnp.where(row % 2 == 0, 1.0, -1.0)

    @pl.when(pl.program_id(0) == 0)
    def _():
        kn_ref[...] = jnp.zeros_like(kn_ref)

    for o in range(HYENA_ORDER):
        ho = _dot_split(h, w3_ref[:, o * 2 * C:(o + 1) * 2 * C])
        fwd = ho[:, :C] * decay
        bwd = ho[:, C:] * decay
        e = fwd + jnp.where(row == 0, 0.0, bwd)
        d = fwd - bwd
        eh, el = _split_bf16(e)
        dh, dl = _split_bf16(d)
        eh_ref[:, o * C:(o + 1) * C] = eh
        el_ref[:, o * C:(o + 1) * C] = el
        dh_ref[:, o * C:(o + 1) * C] = dh
        dlo_ref[:, o * C:(o + 1) * C] = dl
        kn_ref[:, o * C:(o + 1) * C] += jnp.sum(e * sign, axis=0, keepdims=True) * (1.0 / (2 * L))


def _filter_time(L, w1, b1, freq, w2, b2, w3):
    zp, t, deltas = _filter_features(L)
    H = LANES
    pad_c = H - FILTER_HIDDEN
    w1p = jnp.pad(w1.astype(F32), ((0, LANES - FILTER_EMB), (0, pad_c)))
    w2p = jnp.pad(w2.astype(F32), ((0, pad_c), (0, pad_c)))
    w3p = jnp.pad(w3.astype(F32), ((0, pad_c), (0, 0)))
    row = lambda v: jnp.pad(v.astype(F32), (0, pad_c)).reshape(1, H)
    OC = HYENA_ORDER * HYENA_WIDTH
    outs = [jax.ShapeDtypeStruct((L, OC), BF16)] * 4 + [jax.ShapeDtypeStruct((1, OC), F32)]
    tl = DFT_TK
    rows = lambda w: pl.BlockSpec((tl, w), lambda i: (i, 0))
    whole = lambda r, c: pl.BlockSpec((r, c), lambda i: (0, 0))
    return pl.pallas_call(
        functools.partial(_filter_kernel, L=L),
        out_shape=outs,
        grid=(L // tl,),
        in_specs=[rows(H), rows(1), whole(1, HYENA_WIDTH), whole(H, H), whole(1, H), whole(1, H),
                  whole(H, H), whole(1, H), whole(H, 2 * OC)],
        out_specs=[rows(OC)] * 4 + [whole(1, OC)],
        compiler_params=_cparams(("arbitrary",)),
        name="hyena_filter_time",
    )(jnp.asarray(zp), jnp.asarray(t), jnp.asarray(deltas), w1p, row(b1), row(freq), w2p, row(b2), w3p)


def _spectrum_kernel(ca_ref, sa_ref, cb_ref, sb_ref, eh_ref, el_ref, dh_ref, dl_ref,
                     c_ref, s_ref, q1_ref, q2_ref, cf_ref, sf_ref):
    tk, L = c_ref.shape
    cb, sb = cb_ref[...], sb_ref[...]
    for a in range(L // LANES):
        ca = ca_ref[:, a:a + 1]
        sa = sa_ref[:, a:a + 1]
        cf_ref[:, a * LANES:(a + 1) * LANES] = ca * cb - sa * sb
        sf_ref[:, a * LANES:(a + 1) * LANES] = sa * cb + ca * sb
    ch, cl = _split_bf16(cf_ref[...])
    sh, sl = _split_bf16(sf_ref[...])
    c_ref[...] = ch
    s_ref[...] = sh
    kc = _dot(ch, eh_ref[...]) + _dot(cl, eh_ref[...]) + _dot(ch, el_ref[...])
    ks = _dot(sh, dh_ref[...]) + _dot(sl, dh_ref[...]) + _dot(sh, dl_ref[...])
    k_glob = pl.program_id(0) * tk + lax.broadcasted_iota(jnp.int32, kc.shape, 0)
    scale = jnp.where(k_glob == 0, 0.5 / L, 1.0 / L)
    q1_ref[...] = kc * scale
    q2_ref[...] = ks * scale


def _spectrum(L, eh, el, dh, dl):
    ca, sa, cb, sb = _dft_factor_tables(L)
    OC = eh.shape[1]
    tk = DFT_TK
    na = L // LANES
    rows = lambda w: pl.BlockSpec((tk, w), lambda i: (i, 0))
    full = _resident((L, OC), lambda i: (0, 0))
    return pl.pallas_call(
        _spectrum_kernel,
        out_shape=[jax.ShapeDtypeStruct((L, L), BF16), jax.ShapeDtypeStruct((L, L), BF16),
                   jax.ShapeDtypeStruct((L, OC), F32), jax.ShapeDtypeStruct((L, OC), F32)],
        grid=(L // tk,),
        in_specs=[rows(na), rows(na), rows(LANES), rows(LANES), full, full, full, full],
        out_specs=[rows(L), rows(L), rows(OC), rows(OC)],
        scratch_shapes=[pltpu.VMEM((tk, L), F32), pltpu.VMEM((tk, L), F32)],
        compiler_params=_cparams(("parallel",)),
        name="hyena_filter_spectrum",
    )(jnp.asarray(ca), jnp.asarray(sa), jnp.asarray(cb), jnp.asarray(sb), eh, el, dh, dl)


def _hyena_kernel(v_ref, x1_ref, x2_ref, wv_ref, wx1_ref, wx2_ref, bv_ref, bx1_ref, bx2_ref,
                  c_ref, s_ref, q1a_ref, q2a_ref, q1b_ref, q2b_ref, kna_ref, knb_ref,
                  bias_a_ref, bias_b_ref, o_ref, zb_ref, y_ref, z_ref):
    L = v_ref.shape[1]
    kb_n, mb_n = L // HY_KB, L // HY_MB

    def short_conv(u_ref, w_ref, b_ref):
        u = u_ref[0].astype(F32)
        prev, nxt = _shift_rows(u)
        return prev * w_ref[0:1, :] + u * w_ref[1:2, :] + nxt * w_ref[2:3, :] + b_ref[...]

    def long_conv(q1_ref, q2_ref, kn_ref, bias_ref):
        z = z_ref[...]
        zb_ref[...] = z.astype(BF16)
        row = lax.broadcasted_iota(jnp.int32, z.shape, 0)
        sign = jnp.where(row % 2 == 0, 1.0, -1.0)
        z_nyq = jnp.sum(z * sign, axis=0, keepdims=True)
        y_ref[...] = sign * (z_nyq * kn_ref[...]) + z * bias_ref[...]

        def forward(kb):
            r = slice(kb * HY_KB, (kb + 1) * HY_KB)
            return _dot(c_ref[r, :], zb_ref[...]), _dot(s_ref[r, :], zb_ref[...])

        zc, zs = forward(0)
        for kb in range(kb_n):
            if kb + 1 < kb_n:
                nxt = forward(kb + 1)
            r = slice(kb * HY_KB, (kb + 1) * HY_KB)
            q1, q2 = q1_ref[r, :], q2_ref[r, :]
            ya = (zc * q1 - zs * q2).astype(BF16)
            yb = (zc * q2 + zs * q1).astype(BF16)
            for mb in range(mb_n):
                m = slice(mb * HY_MB, (mb + 1) * HY_MB)
                y_ref[m, :] += _dot(c_ref[m, r], ya) + _dot(s_ref[m, r], yb)
            if kb + 1 < kb_n:
                zc, zs = nxt

    z_ref[...] = short_conv(v_ref, wv_ref, bv_ref)
    long_conv(q1a_ref, q2a_ref, kna_ref, bias_a_ref)
    z_ref[...] = short_conv(x1_ref, wx1_ref, bx1_ref) * y_ref[...]
    long_conv(q1b_ref, q2b_ref, knb_ref, bias_b_ref)
    o_ref[0] = (short_conv(x2_ref, wx2_ref, bx2_ref) * y_ref[...]).astype(o_ref.dtype)


def _hyena(proj3, conv_w, conv_b, ctab, stab, q1, q2, knyq, bias):
    B, L, _ = proj3.shape
    C = HYENA_WIDTH
    ct = HY_CT
    nct = C // ct
    conv_w = conv_w.astype(F32)
    conv_b2 = conv_b.astype(F32).reshape(1, 3 * C)
    bias = bias.astype(F32)
    u_spec = lambda g: pl.BlockSpec((1, L, ct), lambda c, b, g=g: (b, 0, g * nct + c))
    w_spec = lambda g: pl.BlockSpec((3, ct), lambda c, b, g=g: (0, g * nct + c))
    b_spec = lambda g: pl.BlockSpec((1, ct), lambda c, b, g=g: (0, g * nct + c))
    tab = _resident((L, L), lambda c, b: (0, 0))
    q_spec = lambda o: _resident((L, ct), lambda c, b, o=o: (0, o * nct + c))
    r_spec = lambda o: pl.BlockSpec((1, ct), lambda c, b, o=o: (0, o * nct + c))
    bias2 = bias.reshape(1, HYENA_ORDER * C)
    return pl.pallas_call(
        _hyena_kernel,
        out_shape=jax.ShapeDtypeStruct((B, L, C), BF16),
        grid=(nct, B),
        in_specs=[u_spec(0), u_spec(1), u_spec(2), w_spec(0), w_spec(1), w_spec(2),
                  b_spec(0), b_spec(1), b_spec(2), tab, tab,
                  q_spec(0), q_spec(0), q_spec(1), q_spec(1), r_spec(0), r_spec(1), r_spec(0), r_spec(1)],
        out_specs=pl.BlockSpec((1, L, ct), lambda c, b: (b, 0, c)),
        scratch_shapes=[pltpu.VMEM((L, ct), BF16), pltpu.VMEM((L, ct), F32), pltpu.VMEM((L, ct), F32)],
        compiler_params=_cparams(("parallel", "parallel")),
        name="hyena_mixer",
    )(proj3, proj3, proj3, conv_w, conv_w, conv_w, conv_b2, conv_b2, conv_b2, ctab, stab,
      q1, q2, q1, q2, knyq, knyq, bias2, bias2)


def _rope(t, cos, sin):
    lane = lax.broadcasted_iota(jnp.int32, t.shape, 1)
    first = (lane % DIFF_HEAD_DIM) < (DIFF_HEAD_DIM // 2)
    up = pltpu.roll(t, LANES - DIFF_HEAD_DIM // 2, 1)
    down = pltpu.roll(t, DIFF_HEAD_DIM // 2, 1)
    rot = jnp.where(first, -up, down)
    return t * cos + rot * sin


def _diff_attn_kernel(q_ref, k_ref, v_ref, cos_ref, sin_ref, lam_ref, g_ref, o_ref, kr_ref, vx_ref,
                      *, lambda_init):
    S, hw = k_ref.shape[1], k_ref.shape[2]
    tq = DA_TQ
    kr_ref[...] = _rope(k_ref[0].astype(F32), cos_ref[...], sin_ref[...]).astype(BF16)
    vx_ref[:, 0:hw] = v_ref[0]
    vx_ref[:, hw:] = jnp.ones((S, hw), BF16)
    lp = lam_ref[...]
    lam = (jnp.exp(jnp.sum(lp[0:1] * lp[1:2], axis=-1, keepdims=True))
           - jnp.exp(jnp.sum(lp[2:3] * lp[3:4], axis=-1, keepdims=True)) + lambda_init)
    scale = DIFF_HEAD_DIM ** -0.5 * LOG2_E

    def scores(r):
        q = (_rope(q_ref[0, r, :].astype(F32), cos_ref[r, :], sin_ref[r, :]) * scale).astype(BF16)
        lane = lax.broadcasted_iota(jnp.int32, q.shape, 1)
        zero = jnp.zeros_like(q)
        qq = jnp.concatenate([jnp.where(lane < DIFF_HEAD_DIM, q, zero),
                              jnp.where(lane >= DIFF_HEAD_DIM, q, zero)], axis=0)
        return lax.dot_general(qq, kr_ref[...], (((1,), (1,)), ((), ())), preferred_element_type=F32)

    def finish(s, r):
        m = jnp.max(s, axis=-1, keepdims=True)
        p = jnp.exp2(s - m).astype(BF16)
        ov = _dot(p, vx_ref[...])
        on = ov[:, :hw] * (1.0 / ov[:, hw:hw + 1])
        o = on[:tq] - lam * on[tq:]
        o = o * lax.rsqrt(jnp.mean(o * o, axis=-1, keepdims=True) + RMS_EPS)
        o_ref[0, r, :] = (o * g_ref[...] * (1.0 - lambda_init)).astype(o_ref.dtype)

    blocks = [slice(i * tq, (i + 1) * tq) for i in range(S // tq)]
    s = scores(blocks[0])
    for i, r in enumerate(blocks):
        if i + 1 < len(blocks):
            nxt = scores(blocks[i + 1])
        finish(s, r)
        if i + 1 < len(blocks):
            s = nxt


def _diff_attn(proj3, lam_params, subln_g, lambda_init, q_col, k_col, v_col):
    B, S, _ = proj3.shape
    hw = 2 * DIFF_HEAD_DIM
    cos, sin = _rope_tables(S)
    cos, sin = jnp.asarray(cos), jnp.asarray(sin)
    qb, kb, vb = q_col // hw, k_col // hw, v_col // hw
    head = lambda c0: pl.BlockSpec((1, S, hw), lambda b, h, c0=c0: (b, 0, c0 + h))
    table = pl.BlockSpec((S, hw), lambda b, h: (0, 0))
    return pl.pallas_call(
        functools.partial(_diff_attn_kernel, lambda_init=lambda_init),
        out_shape=jax.ShapeDtypeStruct((B, S, DIFF_WIDTH), BF16),
        grid=(B, DIFF_HEADS),
        in_specs=[head(qb), head(kb), head(vb), table, table,
                  pl.BlockSpec((4, DIFF_HEAD_DIM), lambda b, h: (0, 0)),
                  pl.BlockSpec((1, hw), lambda b, h: (0, 0))],
        out_specs=pl.BlockSpec((1, S, hw), lambda b, h: (b, 0, h)),
        scratch_shapes=[pltpu.VMEM((S, hw), BF16), pltpu.VMEM((S, 2 * hw), BF16)],
        compiler_params=_cparams(("parallel", "parallel")),
        name="diff_attention",
    )(proj3, proj3, proj3, cos, sin, lam_params.astype(F32), subln_g.astype(F32).reshape(1, hw))


def _mem_attn_kernel(q_ref, kv_ref, o_ref):
    scale = MEM_HEAD_DIM ** -0.5
    d = MEM_HEAD_DIM
    for h in range(MEM_HEADS):
        q = q_ref[0, :, h * d:(h + 1) * d]
        k = kv_ref[0, :, h * d:(h + 1) * d]
        v = kv_ref[0, :, MEM_WIDTH + h * d:MEM_WIDTH + (h + 1) * d]
        s = lax.dot_general(q, k, (((1,), (1,)), ((), ())), preferred_element_type=F32) * scale
        m = jnp.max(s, axis=-1, keepdims=True)
        p = jnp.exp(s - m)
        pn = p * (1.0 / jnp.sum(p, axis=-1, keepdims=True))
        o_ref[0, :, h * d:(h + 1) * d] = _dot(pn.astype(BF16), v).astype(o_ref.dtype)


def _mem_attn(proj3, kv3, q_col):
    B, S, _ = proj3.shape
    M = kv3.shape[1]
    ts = MA_TS
    qb = q_col // MEM_WIDTH
    return pl.pallas_call(
        _mem_attn_kernel,
        out_shape=jax.ShapeDtypeStruct((B, S, MEM_WIDTH), BF16),
        grid=(B, S // ts),
        in_specs=[pl.BlockSpec((1, ts, MEM_WIDTH), lambda b, i: (b, i, qb)),
                  pl.BlockSpec((1, M, 2 * MEM_WIDTH), lambda b, i: (b, 0, 0))],
        out_specs=pl.BlockSpec((1, ts, MEM_WIDTH), lambda b, i: (b, i, 0)),
        compiler_params=_cparams(("parallel", "parallel")),
        name="memory_attention",
    )(proj3, kv3)


def _out_ln_kernel(yh_ref, yd_ref, ym_ref, x_ref, w_ref, g_ref, b_ref, o_ref):
    wh = yh_ref.shape[1]
    wd = yd_ref.shape[1]
    mix = (_dot(yh_ref[...], w_ref[0:wh, :]) + _dot(yd_ref[...], w_ref[wh:wh + wd, :])
           + _dot(ym_ref[...], w_ref[wh + wd:, :]))
    o_ref[...] = _layer_norm(DEEPNORM_ALPHA * x_ref[...] + mix, g_ref[...], b_ref[...])


def _out_ln(yh, yd, ym, x2d, w_out_bf16, g, b):
    M, D = x2d.shape
    tm = OUT_TM
    rows = lambda w: pl.BlockSpec((tm, w), lambda i: (i, 0))
    vec = pl.BlockSpec((1, D), lambda i: (0, 0))
    return pl.pallas_call(
        _out_ln_kernel,
        out_shape=jax.ShapeDtypeStruct((M, D), F32),
        grid=(M // tm,),
        in_specs=[rows(yh.shape[1]), rows(yd.shape[1]), rows(ym.shape[1]), rows(D),
                  _resident(w_out_bf16.shape, lambda i: (0, 0)), vec, vec],
        out_specs=rows(D),
        compiler_params=_cparams(("parallel",)),
        name="out_proj_ln",
    )(yh, yd, ym, x2d, w_out_bf16, g.astype(F32).reshape(1, D), b.astype(F32).reshape(1, D))


def _ffn_kernel(xm_ref, xp_ref, xn_ref, wup_ref, cw_ref, cb_ref, wdn_ref, g_ref, b_ref, o_ref,
                xe_ref, acc_ref):
    tm = xm_ref.shape[1]
    halo = xp_ref.shape[1]
    nchunk = wdn_ref.shape[0]
    i = pl.program_id(1)
    last = pl.num_programs(1) - 1
    xe_ref[0:halo, :] = jnp.where(i == 0, 0.0, xp_ref[0]).astype(BF16)
    xe_ref[halo:halo + tm, :] = xm_ref[0].astype(BF16)
    xe_ref[halo + tm:, :] = jnp.where(i == last, 0.0, xn_ref[0]).astype(BF16)

    def up(j):
        return _dot(xe_ref[...], wup_ref[j]), _dot(xe_ref[...], wup_ref[nchunk + j])

    def conv(hc, j):
        n = hc.shape[0]
        prev = pltpu.roll(hc, 1, 0)[halo:halo + tm]
        nxt = pltpu.roll(hc, n - 1, 0)[halo:halo + tm]
        w = cw_ref[j]
        return prev * w[0:1] + hc[halo:halo + tm] * w[1:2] + nxt * w[2:3] + cb_ref[j]

    h = up(0)
    for j in range(nchunk):
        if j + 1 < nchunk:
            nxt = up(j + 1)
        hg = conv(h[0], j)
        hu = conv(h[1], nchunk + j)
        act = hg * (1.0 / (1.0 + jnp.exp(-hg))) * hu
        d = _dot(act.astype(BF16), wdn_ref[j])
        if j == 0:
            acc_ref[...] = d
        else:
            acc_ref[...] += d
        if j + 1 < nchunk:
            h = nxt
    o_ref[0] = _layer_norm(DEEPNORM_ALPHA * xm_ref[0] + acc_ref[...], g_ref[...], b_ref[...])


def _ffn(x3, w_up, conv_w, conv_b, w_down, g, b):
    B, S, D = x3.shape
    tm, cw, halo = FFN_TM, FFN_CW, FFN_HALO
    nchunk = D_FF // cw
    wup = w_up.astype(BF16).reshape(D, 2 * nchunk, cw).transpose(1, 0, 2)
    cwt = conv_w.astype(F32).reshape(3, 2 * nchunk, cw).transpose(1, 0, 2)
    cbt = conv_b.astype(F32).reshape(2 * nchunk, 1, cw)
    wdn = w_down.astype(BF16).reshape(nchunk, cw, D)
    r = tm // halo
    nblk = S // halo
    vec = pl.BlockSpec((1, D), lambda b_, i: (0, 0))
    return pl.pallas_call(
        _ffn_kernel,
        out_shape=jax.ShapeDtypeStruct((B, S, D), F32),
        grid=(B, S // tm),
        in_specs=[pl.BlockSpec((1, tm, D), lambda b_, i: (b_, i, 0)),
                  pl.BlockSpec((1, halo, D), lambda b_, i: (b_, jnp.maximum(i * r - 1, 0), 0)),
                  pl.BlockSpec((1, halo, D), lambda b_, i: (b_, jnp.minimum((i + 1) * r, nblk - 1), 0)),
                  _resident(wup.shape, lambda b_, i: (0, 0, 0)),
                  _resident(cwt.shape, lambda b_, i: (0, 0, 0)),
                  _resident(cbt.shape, lambda b_, i: (0, 0, 0)),
                  _resident(wdn.shape, lambda b_, i: (0, 0, 0)),
                  vec, vec],
        out_specs=pl.BlockSpec((1, tm, D), lambda b_, i: (b_, i, 0)),
        scratch_shapes=[pltpu.VMEM((tm + 2 * halo, D), BF16), pltpu.VMEM((tm, D), F32)],
        compiler_params=_cparams(("parallel", "parallel")),
        name="conv_gated_mlp_ln",
    )(x3, x3, x3, wup, cwt, cbt, wdn, g.astype(F32).reshape(1, D), b.astype(F32).reshape(1, D))


def kernel(x, mem, w_in, hy_conv_w, hy_conv_b, hy_w1, hy_b1, hy_freq, hy_w2, hy_b2, hy_w3, hy_bias,
           diff_lambda, diff_subln_g, mem_w_kv, w_out, ln1_g, ln1_b, ffn_w_up, ffn_conv_w, ffn_conv_b,
           ffn_w_down, ln2_g, ln2_b):
    B, S, D = x.shape
    M = mem.shape[1]
    s1 = 3 * HYENA_WIDTH
    q_col, k_col, v_col = s1, s1 + DIFF_WIDTH, s1 + 2 * DIFF_WIDTH
    mq_col = s1 + 3 * DIFF_WIDTH
    for l in range(DEPTH):
        lambda_init = 0.8 - 0.6 * math.exp(-0.3 * l)
        x2d = x.reshape(B * S, D)
        proj3 = _proj(x2d, w_in[l].astype(BF16), "in_proj").reshape(B, S, -1)
        kv3 = _proj(mem.reshape(B * M, D), mem_w_kv[l].astype(BF16), "mem_kv_proj").reshape(B, M, -1)

        eh, el, dh, dl, knyq = _filter_time(S, hy_w1[l], hy_b1[l], hy_freq[l], hy_w2[l], hy_b2[l], hy_w3[l])
        ctab, stab, q1, q2 = _spectrum(S, eh, el, dh, dl)
        y_h = _hyena(proj3, hy_conv_w[l], hy_conv_b[l], ctab, stab, q1, q2, knyq, hy_bias[l])
        y_d = _diff_attn(proj3, diff_lambda[l], diff_subln_g[l], lambda_init, q_col, k_col, v_col)
        y_m = _mem_attn(proj3, kv3, mq_col)

        x1 = _out_ln(y_h.reshape(B * S, -1), y_d.reshape(B * S, -1), y_m.reshape(B * S, -1), x2d,
                     w_out[l].astype(BF16), ln1_g[l], ln1_b[l])
        x = _ffn(x1.reshape(B, S, D), ffn_w_up[l], ffn_conv_w[l], ffn_conv_b[l], ffn_w_down[l],
                 ln2_g[l], ln2_b[l])
    return x
```

```python
import functools
import math

import jax
import jax.numpy as jnp
import numpy as np
from jax import lax
from jax.experimental import pallas as pl
from jax.experimental.pallas import tpu as pltpu

F32 = jnp.float32
BF16 = jnp.bfloat16

HYENA_WIDTH = 512
HYENA_ORDER = 2
FILTER_EMB = 33
FILTER_HIDDEN = 64
HYENA_TARGET = 1e-2
HYENA_MIN_DECAY = math.log(HYENA_TARGET) / 0.3
HYENA_MAX_DECAY = math.log(HYENA_TARGET) / 1.5
DIFF_HEADS = 4
DIFF_HEAD_DIM = 64
DIFF_WIDTH = DIFF_HEADS * 2 * DIFF_HEAD_DIM
MEM_HEADS = 4
MEM_HEAD_DIM = 128
MEM_WIDTH = MEM_HEADS * MEM_HEAD_DIM
D_FF = 2816
ROPE_THETA = 10000.0
LN_EPS = 1e-5
RMS_EPS = 1e-5
DEPTH = 1
DEEPNORM_ALPHA = (2.0 * DEPTH) ** 0.25
LOG2_E = 1.4426950408889634

LANES = 128
SUBLANES = 8
VMEM_LIMIT_BYTES = 56 * 1024 * 1024

PROJ_TM = 1024
PROJ_TN = 512
HY_CT = 256
HY_KB = 512
HY_MB = 512
DFT_TK = 256
DA_TQ = 512
DA_ONES_ROWS = 2 * SUBLANES
MA_TS = 512
OUT_TM = 512
FFN_TM = 512
FFN_CW = 256
FFN_HALO = SUBLANES


def _cparams(sem):
    return pltpu.CompilerParams(dimension_semantics=sem, vmem_limit_bytes=VMEM_LIMIT_BYTES)


def _resident(block_shape, index_map):
    return pl.BlockSpec(block_shape, index_map, pipeline_mode=pl.Buffered(1))


@functools.lru_cache(maxsize=None)
def _dft_factor_tables(L):
    k = np.arange(L, dtype=np.int64)[:, None]
    a = np.arange(L // LANES, dtype=np.int64)[None, :]
    b = np.arange(LANES, dtype=np.int64)[None, :]
    ang_a = np.pi * ((k * a * LANES) % (2 * L)).astype(np.float64) / L
    ang_b = np.pi * ((k * b) % (2 * L)).astype(np.float64) / L
    f = lambda v: np.asarray(v, dtype=np.float32)
    return f(np.cos(ang_a)), f(np.sin(ang_a)), f(np.cos(ang_b)), f(np.sin(ang_b))


@functools.lru_cache(maxsize=None)
def _filter_features(L):
    t = np.linspace(0.0, 1.0, L, dtype=np.float64)[:, None]
    bands = (FILTER_EMB - 1) // 2
    fr = np.linspace(1e-4, bands - 1, bands, dtype=np.float64)[None, :]
    w = 2.0 * np.pi * np.arange(L, dtype=np.float64)[:, None] / L
    z = np.concatenate([t, np.cos(fr * w), -np.sin(fr * w)], axis=-1)
    zp = np.zeros((L, LANES), np.float64)
    zp[:, :FILTER_EMB] = z
    deltas = np.abs(np.linspace(HYENA_MIN_DECAY, HYENA_MAX_DECAY, HYENA_WIDTH, dtype=np.float64))[None, :]
    return np.asarray(zp, np.float32), np.asarray(t, np.float32), np.asarray(deltas, np.float32)


@functools.lru_cache(maxsize=None)
def _rope_tables(S):
    d = DIFF_HEAD_DIM
    inv_freq = ROPE_THETA ** (-np.arange(0, d, 2, dtype=np.float64) / d)
    ang = np.arange(S, dtype=np.float64)[:, None] * inv_freq[None, :]
    ang = np.concatenate([ang, ang, ang, ang], axis=-1)
    return np.asarray(np.cos(ang), np.float32), np.asarray(np.sin(ang), np.float32)


def _split_bf16(a):
    hi = a.astype(BF16)
    lo = (a - hi.astype(F32)).astype(BF16)
    return hi, lo


def _dot(a, b):
    return jnp.dot(a, b, preferred_element_type=F32)


def _dot_split(a, b):
    ah, al = _split_bf16(a)
    bh, bl = _split_bf16(b)
    return _dot(ah, bh) + _dot(al, bh) + _dot(ah, bl)


def _shift_rows(u):
    n = u.shape[0]
    row = lax.broadcasted_iota(jnp.int32, u.shape, 0)
    prev = jnp.where(row == 0, 0.0, pltpu.roll(u, 1, 0))
    nxt = jnp.where(row == n - 1, 0.0, pltpu.roll(u, n - 1, 0))
    return prev, nxt


def _layer_norm(r, g, b):
    mu = jnp.mean(r, axis=-1, keepdims=True)
    rc = r - mu
    var = jnp.mean(rc * rc, axis=-1, keepdims=True)
    return rc * lax.rsqrt(var + LN_EPS) * g + b


def _proj_kernel(x_ref, w_ref, o_ref, xb_ref):
    xb_ref[...] = x_ref[...].astype(BF16)
    n = w_ref.shape[1]
    tn = min(PROJ_TN, n)
    for j in range(n // tn):
        c = slice(j * tn, (j + 1) * tn)
        o_ref[:, c] = _dot(xb_ref[...], w_ref[:, c]).astype(o_ref.dtype)


def _proj(x2d, w_bf16, name):
    M, K = x2d.shape
    N = w_bf16.shape[1]
    tm = min(PROJ_TM, M)
    return pl.pallas_call(
        _proj_kernel,
        out_shape=jax.ShapeDtypeStruct((M, N), BF16),
        grid=(M // tm,),
        in_specs=[pl.BlockSpec((tm, K), lambda i: (i, 0)),
                  _resident((K, N), lambda i: (0, 0))],
        out_specs=pl.BlockSpec((tm, N), lambda i: (i, 0)),
        scratch_shapes=[pltpu.VMEM((tm, K), BF16)],
        compiler_params=_cparams(("parallel",)),
        name=name,
    )(x2d, w_bf16)


def _filter_kernel(z_ref, t_ref, dl_ref, w1_ref, b1_ref, fq_ref, w2_ref, b2_ref, w3_ref,
                   eh_ref, el_ref, dh_ref, dlo_ref, kn_ref, *, L):
    tl = z_ref.shape[0]
    C = HYENA_WIDTH
    fq = fq_ref[...]
    h = jnp.sin(fq * (_dot_split(z_ref[...], w1_ref[...]) + b1_ref[...]))
    h = jnp.sin(fq * (_dot_split(h, w2_ref[...]) + b2_ref[...]))
    decay = jnp.exp(-t_ref[...] * dl_ref[...])
    row = pl.program_id(0) * tl + lax.broadcasted_iota(jnp.int32, (tl, C), 0)
    sign = jnp.where(row % 2 == 0, 1.0, -1.0)

    @pl.when(pl.program_id(0) == 0)
    def _():
        kn_ref[...] = jnp.zeros_like(kn_ref)

    for o in range(HYENA_ORDER):
        ho = _dot_split(h, w3_ref[:, o * 2 * C:(o + 1) * 2 * C])
        fwd = ho[:, :C] * decay
        bwd = ho[:, C:] * decay
        e = fwd + jnp.where(row == 0, 0.0, bwd)
        d = fwd - bwd
        eh, el = _split_bf16(e)
        dh, dl = _split_bf16(d)
        eh_ref[:, o * C:(o + 1) * C] = eh
        el_ref[:, o * C:(o + 1) * C] = el
        dh_ref[:, o * C:(o + 1) * C] = dh
        dlo_ref[:, o * C:(o + 1) * C] = dl
        kn_ref[:, o * C:(o + 1) * C] += jnp.sum(e * sign, axis=0, keepdims=True) * (1.0 / (2 * L))


def _filter_time(L, w1, b1, freq, w2, b2, w3):
    zp, t, deltas = _filter_features(L)
    H = LANES
    pad_c = H - FILTER_HIDDEN
    w1p = jnp.pad(w1.astype(F32), ((0, LANES - FILTER_EMB), (0, pad_c)))
    w2p = jnp.pad(w2.astype(F32), ((0, pad_c), (0, pad_c)))
    w3p = jnp.pad(w3.astype(F32), ((0, pad_c), (0, 0)))
    row = lambda v: jnp.pad(v.astype(F32), (0, pad_c)).reshape(1, H)
    OC = HYENA_ORDER * HYENA_WIDTH
    outs = [jax.ShapeDtypeStruct((L, OC), BF16)] * 4 + [jax.ShapeDtypeStruct((1, OC), F32)]
    tl = DFT_TK
    rows = lambda w: pl.BlockSpec((tl, w), lambda i: (i, 0))
    whole = lambda r, c: pl.BlockSpec((r, c), lambda i: (0, 0))
    return pl.pallas_call(
        functools.partial(_filter_kernel, L=L),
        out_shape=outs,
        grid=(L // tl,),
        in_specs=[rows(H), rows(1), whole(1, HYENA_WIDTH), whole(H, H), whole(1, H), whole(1, H),
                  whole(H, H), whole(1, H), whole(H, 2 * OC)],
        out_specs=[rows(OC)] * 4 + [whole(1, OC)],
        compiler_params=_cparams(("arbitrary",)),
        name="hyena_filter_time",
    )(jnp.asarray(zp), jnp.asarray(t), jnp.asarray(deltas), w1p, row(b1), row(freq), w2p, row(b2), w3p)


def _spectrum_kernel(ca_ref, sa_ref, cb_ref, sb_ref, eh_ref, el_ref, dh_ref, dl_ref,
                     c_ref, s_ref, q1_ref, q2_ref, cf_ref, sf_ref):
    tk, L = c_ref.shape
    cb, sb = cb_ref[...], sb_ref[...]
    for a in range(L // LANES):
        ca = ca_ref[:, a:a + 1]
        sa = sa_ref[:, a:a + 1]
        cf_ref[:, a * LANES:(a + 1) * LANES] = ca * cb - sa * sb
        sf_ref[:, a * LANES:(a + 1) * LANES] = sa * cb + ca * sb
    ch, cl = _split_bf16(cf_ref[...])
    sh, sl = _split_bf16(sf_ref[...])
    c_ref[...] = ch
    s_ref[...] = sh
    kc = _dot(ch, eh_ref[...]) + _dot(cl, eh_ref[...]) + _dot(ch, el_ref[...])
    ks = _dot(sh, dh_ref[...]) + _dot(sl, dh_ref[...]) + _dot(sh, dl_ref[...])
    k_glob = pl.program_id(0) * tk + lax.broadcasted_iota(jnp.int32, kc.shape, 0)
    scale = jnp.where(k_glob == 0, 0.5 / L, 1.0 / L)
    q1_ref[...] = kc * scale
    q2_ref[...] = ks * scale


def _spectrum(L, eh, el, dh, dl):
    ca, sa, cb, sb = _dft_factor_tables(L)
    OC = eh.shape[1]
    tk = DFT_TK
    na = L // LANES
    rows = lambda w: pl.BlockSpec((tk, w), lambda i: (i, 0))
    full = _resident((L, OC), lambda i: (0, 0))
    return pl.pallas_call(
        _spectrum_kernel,
        out_shape=[jax.ShapeDtypeStruct((L, L), BF16), jax.ShapeDtypeStruct((L, L), BF16),
                   jax.ShapeDtypeStruct((L, OC), F32), jax.ShapeDtypeStruct((L, OC), F32)],
        grid=(L // tk,),
        in_specs=[rows(na), rows(na), rows(LANES), rows(LANES), full, full, full, full],
        out_specs=[rows(L), rows(L), rows(OC), rows(OC)],
        scratch_shapes=[pltpu.VMEM((tk, L), F32), pltpu.VMEM((tk, L), F32)],
        compiler_params=_cparams(("parallel",)),
        name="hyena_filter_spectrum",
    )(jnp.asarray(ca), jnp.asarray(sa), jnp.asarray(cb), jnp.asarray(sb), eh, el, dh, dl)


def _hyena_kernel(v_ref, x1_ref, x2_ref, wv_ref, wx1_ref, wx2_ref, bv_ref, bx1_ref, bx2_ref,
                  c_ref, s_ref, q1a_ref, q2a_ref, q1b_ref, q2b_ref, kna_ref, knb_ref,
                  bias_a_ref, bias_b_ref, o_ref, zb_ref, y_ref, z_ref):
    L = v_ref.shape[1]
    kb_n, mb_n = L // HY_KB, L // HY_MB

    def short_conv(u_ref, w_ref, b_ref):
        u = u_ref[0].astype(F32)
        prev, nxt = _shift_rows(u)
        return prev * w_ref[0:1, :] + u * w_ref[1:2, :] + nxt * w_ref[2:3, :] + b_ref[...]

    def long_conv(q1_ref, q2_ref, kn_ref, bias_ref):
        z = z_ref[...]
        zb_ref[...] = z.astype(BF16)
        row = lax.broadcasted_iota(jnp.int32, z.shape, 0)
        sign = jnp.where(row % 2 == 0, 1.0, -1.0)
        z_nyq = jnp.sum(z * sign, axis=0, keepdims=True)
        y_ref[...] = sign * (z_nyq * kn_ref[...]) + z * bias_ref[...]

        def forward(kb):
            r = slice(kb * HY_KB, (kb + 1) * HY_KB)
            return _dot(c_ref[r, :], zb_ref[...]), _dot(s_ref[r, :], zb_ref[...])

        zc, zs = forward(0)
        for kb in range(kb_n):
            if kb + 1 < kb_n:
                nxt = forward(kb + 1)
            r = slice(kb * HY_KB, (kb + 1) * HY_KB)
            q1, q2 = q1_ref[r, :], q2_ref[r, :]
            ya = (zc * q1 - zs * q2).astype(BF16)
            yb = (zc * q2 + zs * q1).astype(BF16)
            for mb in range(mb_n):
                m = slice(mb * HY_MB, (mb + 1) * HY_MB)
                y_ref[m, :] += _dot(c_ref[m, r], ya) + _dot(s_ref[m, r], yb)
            if kb + 1 < kb_n:
                zc, zs = nxt

    z_ref[...] = short_conv(v_ref, wv_ref, bv_ref)
    long_conv(q1a_ref, q2a_ref, kna_ref, bias_a_ref)
    z_ref[...] = short_conv(x1_ref, wx1_ref, bx1_ref) * y_ref[...]
    long_conv(q1b_ref, q2b_ref, knb_ref, bias_b_ref)
    o_ref[0] = (short_conv(x2_ref, wx2_ref, bx2_ref) * y_ref[...]).astype(o_ref.dtype)


def _hyena(proj3, conv_w, conv_b, ctab, stab, q1, q2, knyq, bias):
    B, L, _ = proj3.shape
    C = HYENA_WIDTH
    ct = HY_CT
    nct = C // ct
    conv_w = conv_w.astype(F32)
    conv_b2 = conv_b.astype(F32).reshape(1, 3 * C)
    bias = bias.astype(F32)
    u_spec = lambda g: pl.BlockSpec((1, L, ct), lambda c, b, g=g: (b, 0, g * nct + c))
    w_spec = lambda g: pl.BlockSpec((3, ct), lambda c, b, g=g: (0, g * nct + c))
    b_spec = lambda g: pl.BlockSpec((1, ct), lambda c, b, g=g: (0, g * nct + c))
    tab = _resident((L, L), lambda c, b: (0, 0))
    q_spec = lambda o: _resident((L, ct), lambda c, b, o=o: (0, o * nct + c))
    r_spec = lambda o: pl.BlockSpec((1, ct), lambda c, b, o=o: (0, o * nct + c))
    bias2 = bias.reshape(1, HYENA_ORDER * C)
    return pl.pallas_call(
        _hyena_kernel,
        out_shape=jax.ShapeDtypeStruct((B, L, C), BF16),
        grid=(nct, B),
        in_specs=[u_spec(0), u_spec(1), u_spec(2), w_spec(0), w_spec(1), w_spec(2),
                  b_spec(0), b_spec(1), b_spec(2), tab, tab,
                  q_spec(0), q_spec(0), q_spec(1), q_spec(1), r_spec(0), r_spec(1), r_spec(0), r_spec(1)],
        out_specs=pl.BlockSpec((1, L, ct), lambda c, b: (b, 0, c)),
        scratch_shapes=[pltpu.VMEM((L, ct), BF16), pltpu.VMEM((L, ct), F32), pltpu.VMEM((L, ct), F32)],
        compiler_params=_cparams(("parallel", "parallel")),
        name="hyena_mixer",
    )(proj3, proj3, proj3, conv_w, conv_w, conv_w, conv_b2, conv_b2, conv_b2, ctab, stab,
      q1, q2, q1, q2, knyq, knyq, bias2, bias2)


def _rope(t, cos, sin):
    lane = lax.broadcasted_iota(jnp.int32, t.shape, 1)
    first = (lane % DIFF_HEAD_DIM) < (DIFF_HEAD_DIM // 2)
    up = pltpu.roll(t, LANES - DIFF_HEAD_DIM // 2, 1)
    down = pltpu.roll(t, DIFF_HEAD_DIM // 2, 1)
    rot = jnp.where(first, -up, down)
    return t * cos + rot * sin


def _diff_attn_kernel(q_ref, k_ref, v_ref, cos_ref, sin_ref, lam_ref, g_ref, o_ref, kr_ref, vt_ref,
                      *, lambda_init):
    S, hw = k_ref.shape[1], k_ref.shape[2]
    tq = DA_TQ
    kr_ref[...] = _rope(k_ref[0].astype(F32), cos_ref[...], sin_ref[...]).astype(BF16)
    vt_ref[0:hw, :] = v_ref[0].astype(F32).T.astype(BF16)
    vt_ref[hw:, :] = jnp.ones((DA_ONES_ROWS, S), BF16)
    lp = lam_ref[...]
    lam = (jnp.exp(jnp.sum(lp[0:1] * lp[1:2], axis=-1, keepdims=True))
           - jnp.exp(jnp.sum(lp[2:3] * lp[3:4], axis=-1, keepdims=True)) + lambda_init)
    scale = DIFF_HEAD_DIM ** -0.5 * LOG2_E

    def scores(r):
        q = (_rope(q_ref[0, r, :].astype(F32), cos_ref[r, :], sin_ref[r, :]) * scale).astype(BF16)
        lane = lax.broadcasted_iota(jnp.int32, q.shape, 1)
        zero = jnp.zeros_like(q)
        qq = jnp.concatenate([jnp.where(lane < DIFF_HEAD_DIM, q, zero),
                              jnp.where(lane >= DIFF_HEAD_DIM, q, zero)], axis=0)
        return lax.dot_general(kr_ref[...], qq, (((1,), (1,)), ((), ())), preferred_element_type=F32)

    def finish(st, r):
        m = jnp.max(st, axis=0, keepdims=True)
        p = jnp.exp2(st - m).astype(BF16)
        ov = _dot(vt_ref[...], p)
        on = ov[:hw, :] * (1.0 / ov[hw:hw + 1, :])
        o = on[:, :tq] - lam * on[:, tq:]
        o = o * lax.rsqrt(jnp.mean(o * o, axis=0, keepdims=True) + RMS_EPS)
        o = o * g_ref[...] * (1.0 - lambda_init)
        o_ref[0, r, :] = o.T.astype(o_ref.dtype)

    blocks = [slice(i * tq, (i + 1) * tq) for i in range(S // tq)]
    s = scores(blocks[0])
    for i, r in enumerate(blocks):
        if i + 1 < len(blocks):
            nxt = scores(blocks[i + 1])
        finish(s, r)
        if i + 1 < len(blocks):
            s = nxt


def _diff_attn(proj3, lam_params, subln_g, lambda_init, q_col, k_col, v_col):
    B, S, _ = proj3.shape
    hw = 2 * DIFF_HEAD_DIM
    cos, sin = _rope_tables(S)
    cos, sin = jnp.asarray(cos), jnp.asarray(sin)
    qb, kb, vb = q_col // hw, k_col // hw, v_col // hw
    head = lambda c0: pl.BlockSpec((1, S, hw), lambda b, h, c0=c0: (b, 0, c0 + h))
    table = pl.BlockSpec((S, hw), lambda b, h: (0, 0))
    return pl.pallas_call(
        functools.partial(_diff_attn_kernel, lambda_init=lambda_init),
        out_shape=jax.ShapeDtypeStruct((B, S, DIFF_WIDTH), BF16),
        grid=(B, DIFF_HEADS),
        in_specs=[head(qb), head(kb), head(vb), table, table,
                  pl.BlockSpec((4, DIFF_HEAD_DIM), lambda b, h: (0, 0)),
                  pl.BlockSpec((hw, 1), lambda b, h: (0, 0))],
        out_specs=pl.BlockSpec((1, S, hw), lambda b, h: (b, 0, h)),
        scratch_shapes=[pltpu.VMEM((S, hw), BF16), pltpu.VMEM((hw + DA_ONES_ROWS, S), BF16)],
        compiler_params=_cparams(("parallel", "parallel")),
        name="diff_attention",
    )(proj3, proj3, proj3, cos, sin, lam_params.astype(F32), subln_g.astype(F32).reshape(hw, 1))


def _mem_attn_kernel(q_ref, kv_ref, o_ref):
    scale = MEM_HEAD_DIM ** -0.5
    d = MEM_HEAD_DIM
    for h in range(MEM_HEADS):
        q = q_ref[0, :, h * d:(h + 1) * d]
        k = kv_ref[0, :, h * d:(h + 1) * d]
        v = kv_ref[0, :, MEM_WIDTH + h * d:MEM_WIDTH + (h + 1) * d]
        s = lax.dot_general(q, k, (((1,), (1,)), ((), ())), preferred_element_type=F32) * scale
        m = jnp.max(s, axis=-1, keepdims=True)
        p = jnp.exp(s - m)
        pn = p * (1.0 / jnp.sum(p, axis=-1, keepdims=True))
        o_ref[0, :, h * d:(h + 1) * d] = _dot(pn.astype(BF16), v).astype(o_ref.dtype)


def _mem_attn(proj3, kv3, q_col):
    B, S, _ = proj3.shape
    M = kv3.shape[1]
    ts = MA_TS
    qb = q_col // MEM_WIDTH
    return pl.pallas_call(
        _mem_attn_kernel,
        out_shape=jax.ShapeDtypeStruct((B, S, MEM_WIDTH), BF16),
        grid=(B, S // ts),
        in_specs=[pl.BlockSpec((1, ts, MEM_WIDTH), lambda b, i: (b, i, qb)),
                  pl.BlockSpec((1, M, 2 * MEM_WIDTH), lambda b, i: (b, 0, 0))],
        out_specs=pl.BlockSpec((1, ts, MEM_WIDTH), lambda b, i: (b, i, 0)),
        compiler_params=_cparams(("parallel", "parallel")),
        name="memory_attention",
    )(proj3, kv3)


def _out_ln_kernel(yh_ref, yd_ref, ym_ref, x_ref, w_ref, g_ref, b_ref, o_ref):
    wh = yh_ref.shape[1]
    wd = yd_ref.shape[1]
    mix = (_dot(yh_ref[...], w_ref[0:wh, :]) + _dot(yd_ref[...], w_ref[wh:wh + wd, :])
           + _dot(ym_ref[...], w_ref[wh + wd:, :]))
    o_ref[...] = _layer_norm(DEEPNORM_ALPHA * x_ref[...] + mix, g_ref[...], b_ref[...])


def _out_ln(yh, yd, ym, x2d, w_out_bf16, g, b):
    M, D = x2d.shape
    tm = OUT_TM
    rows = lambda w: pl.BlockSpec((tm, w), lambda i: (i, 0))
    vec = pl.BlockSpec((1, D), lambda i: (0, 0))
    return pl.pallas_call(
        _out_ln_kernel,
        out_shape=jax.ShapeDtypeStruct((M, D), F32),
        grid=(M // tm,),
        in_specs=[rows(yh.shape[1]), rows(yd.shape[1]), rows(ym.shape[1]), rows(D),
                  _resident(w_out_bf16.shape, lambda i: (0, 0)), vec, vec],
        out_specs=rows(D),
        compiler_params=_cparams(("parallel",)),
        name="out_proj_ln",
    )(yh, yd, ym, x2d, w_out_bf16, g.astype(F32).reshape(1, D), b.astype(F32).reshape(1, D))


def _ffn_kernel(xm_ref, xp_ref, xn_ref, wgu_ref, cw_ref, cb_ref, wdn_ref, g_ref, b_ref, o_ref,
                xe_ref, act_ref):
    tm = xm_ref.shape[1]
    halo = xp_ref.shape[1]
    nchunk = wgu_ref.shape[0]
    cw = wgu_ref.shape[2] // 2
    i = pl.program_id(1)
    last = pl.num_programs(1) - 1
    xe_ref[0:halo, :] = jnp.where(i == 0, 0.0, xp_ref[0]).astype(BF16)
    xe_ref[halo:halo + tm, :] = xm_ref[0].astype(BF16)
    xe_ref[halo + tm:, :] = jnp.where(i == last, 0.0, xn_ref[0]).astype(BF16)

    def up(j):
        return _dot(xe_ref[...], wgu_ref[j])

    def conv(hc, j):
        n = hc.shape[0]
        prev = pltpu.roll(hc, 1, 0)[halo:halo + tm]
        nxt = pltpu.roll(hc, n - 1, 0)[halo:halo + tm]
        w = cw_ref[j]
        return prev * w[0:1] + hc[halo:halo + tm] * w[1:2] + nxt * w[2:3] + cb_ref[j]

    h = up(0)
    for j in range(nchunk):
        if j + 1 < nchunk:
            nxt = up(j + 1)
        c = conv(h, j)
        hg, hu = c[:, :cw], c[:, cw:]
        act = hg * (1.0 / (1.0 + jnp.exp(-hg))) * hu
        act_ref[:, j * cw:(j + 1) * cw] = act.astype(BF16)
        if j + 1 < nchunk:
            h = nxt
    y = _dot(act_ref[...], wdn_ref[...])
    o_ref[0] = _layer_norm(DEEPNORM_ALPHA * xm_ref[0] + y, g_ref[...], b_ref[...])


def _ffn(x3, w_up, conv_w, conv_b, w_down, g, b):
    B, S, D = x3.shape
    tm, cw, halo = FFN_TM, FFN_CW, FFN_HALO
    nchunk = D_FF // cw
    pair = lambda a: jnp.concatenate([a[..., :D_FF].reshape(a.shape[0], nchunk, cw),
                                      a[..., D_FF:].reshape(a.shape[0], nchunk, cw)], axis=-1).transpose(1, 0, 2)
    wgu = pair(w_up.astype(BF16))
    cwt = pair(conv_w.astype(F32))
    cbt = pair(conv_b.astype(F32).reshape(1, -1))
    wdn = w_down.astype(BF16)
    r = tm // halo
    nblk = S // halo
    vec = pl.BlockSpec((1, D), lambda b_, i: (0, 0))
    return pl.pallas_call(
        _ffn_kernel,
        out_shape=jax.ShapeDtypeStruct((B, S, D), F32),
        grid=(B, S // tm),
        in_specs=[pl.BlockSpec((1, tm, D), lambda b_, i: (b_, i, 0)),
                  pl.BlockSpec((1, halo, D), lambda b_, i: (b_, jnp.maximum(i * r - 1, 0), 0)),
                  pl.BlockSpec((1, halo, D), lambda b_, i: (b_, jnp.minimum((i + 1) * r, nblk - 1), 0)),
                  _resident(wgu.shape, lambda b_, i: (0, 0, 0)),
                  _resident(cwt.shape, lambda b_, i: (0, 0, 0)),
                  _resident(cbt.shape, lambda b_, i: (0, 0, 0)),
                  _resident(wdn.shape, lambda b_, i: (0, 0)),
                  vec, vec],
        out_specs=pl.BlockSpec((1, tm, D), lambda b_, i: (b_, i, 0)),
        scratch_shapes=[pltpu.VMEM((tm + 2 * halo, D), BF16), pltpu.VMEM((tm, D_FF), BF16)],
        compiler_params=_cparams(("parallel", "parallel")),
        name="conv_gated_mlp_ln",
    )(x3, x3, x3, wgu, cwt, cbt, wdn, g.astype(F32).reshape(1, D), b.astype(F32).reshape(1, D))


def kernel(x, mem, w_in, hy_conv_w, hy_conv_b, hy_w1, hy_b1, hy_freq, hy_w2, hy_b2, hy_w3, hy_bias,
           diff_lambda, diff_subln_g, mem_w_kv, w_out, ln1_g, ln1_b, ffn_w_up, ffn_conv_w, ffn_conv_b,
           ffn_w_down, ln2_g, ln2_b):
    B, S, D = x.shape
    M = mem.shape[1]
    s1 = 3 * HYENA_WIDTH
    q_col, k_col, v_col = s1, s1 + DIFF_WIDTH, s1 + 2 * DIFF_WIDTH
    mq_col = s1 + 3 * DIFF_WIDTH
    for l in range(DEPTH):
        lambda_init = 0.8 - 0.6 * math.exp(-0.3 * l)
        x2d = x.reshape(B * S, D)
        proj3 = _proj(x2d, w_in[l].astype(BF16), "in_proj").reshape(B, S, -1)
        kv3 = _proj(mem.reshape(B * M, D), mem_w_kv[l].astype(BF16), "mem_kv_proj").reshape(B, M, -1)

        eh, el, dh, dl, knyq = _filter_time(S, hy_w1[l], hy_b1[l], hy_freq[l], hy_w2[l], hy_b2[l], hy_w3[l])
        ctab, stab, q1, q2 = _spectrum(S, eh, el, dh, dl)
        y_h = _hyena(proj3, hy_conv_w[l], hy_conv_b[l], ctab, stab, q1, q2, knyq, hy_bias[l])
        y_d = _diff_attn(proj3, diff_lambda[l], diff_subln_g[l], lambda_init, q_col, k_col, v_col)
        y_m = _mem_attn(proj3, kv3, mq_col)

        x1 = _out_ln(y_h.reshape(B * S, -1), y_d.reshape(B * S, -1), y_m.reshape(B * S, -1), x2d,
                     w_out[l].astype(BF16), ln1_g[l], ln1_b[l])
        x = _ffn(x1.reshape(B, S, D), ffn_w_up[l], ffn_conv_w[l], ffn_conv_b[l], ffn_w_down[l],
                 ln2_g[l], ln2_b[l])
    return x
```

```python
import functools
import math

import jax
import jax.numpy as jnp
import numpy as np
from jax import lax
from jax.experimental import pallas as pl
from jax.experimental.pallas import tpu as pltpu

F32 = jnp.float32
BF16 = jnp.bfloat16

HYENA_WIDTH = 512
HYENA_ORDER = 2
FILTER_EMB = 33
FILTER_HIDDEN = 64
HYENA_TARGET = 1e-2
HYENA_MIN_DECAY = math.log(HYENA_TARGET) / 0.3
HYENA_MAX_DECAY = math.log(HYENA_TARGET) / 1.5
DIFF_HEADS = 4
DIFF_HEAD_DIM = 64
DIFF_WIDTH = DIFF_HEADS * 2 * DIFF_HEAD_DIM
MEM_HEADS = 4
MEM_HEAD_DIM = 128
MEM_WIDTH = MEM_HEADS * MEM_HEAD_DIM
D_FF = 2816
ROPE_THETA = 10000.0
LN_EPS = 1e-5
RMS_EPS = 1e-5
DEPTH = 1
DEEPNORM_ALPHA = (2.0 * DEPTH) ** 0.25
LOG2_E = 1.4426950408889634

LANES = 128
SUBLANES = 8
VMEM_LIMIT_BYTES = 56 * 1024 * 1024

PROJ_TM = 1024
PROJ_TN = 512
HY_CT = 256
HY_KB = 512
HY_MB = 512
DFT_TK = 256
DA_TQ = 512
DA_ONES_ROWS = 2 * SUBLANES
OUT_TM = 1024
OUT_SUB = 512
FFN_TM = 512
FFN_CW = 256
FFN_HALO = SUBLANES


def _cparams(sem):
    return pltpu.CompilerParams(dimension_semantics=sem, vmem_limit_bytes=VMEM_LIMIT_BYTES)


def _resident(block_shape, index_map):
    return pl.BlockSpec(block_shape, index_map, pipeline_mode=pl.Buffered(1))


@functools.lru_cache(maxsize=None)
def _dft_factor_tables(L):
    k = np.arange(L, dtype=np.int64)[:, None]
    a = np.arange(L // LANES, dtype=np.int64)[None, :]
    b = np.arange(LANES, dtype=np.int64)[None, :]
    ang_a = np.pi * ((k * a * LANES) % (2 * L)).astype(np.float64) / L
    ang_b = np.pi * ((k * b) % (2 * L)).astype(np.float64) / L
    f = lambda v: np.asarray(v, dtype=np.float32)
    return f(np.cos(ang_a)), f(np.sin(ang_a)), f(np.cos(ang_b)), f(np.sin(ang_b))


@functools.lru_cache(maxsize=None)
def _filter_features(L):
    t = np.linspace(0.0, 1.0, L, dtype=np.float64)[:, None]
    bands = (FILTER_EMB - 1) // 2
    fr = np.linspace(1e-4, bands - 1, bands, dtype=np.float64)[None, :]
    w = 2.0 * np.pi * np.arange(L, dtype=np.float64)[:, None] / L
    z = np.concatenate([t, np.cos(fr * w), -np.sin(fr * w)], axis=-1)
    zp = np.zeros((L, LANES), np.float64)
    zp[:, :FILTER_EMB] = z
    deltas = np.abs(np.linspace(HYENA_MIN_DECAY, HYENA_MAX_DECAY, HYENA_WIDTH, dtype=np.float64))[None, :]
    return np.asarray(zp, np.float32), np.asarray(t, np.float32), np.asarray(deltas, np.float32)


@functools.lru_cache(maxsize=None)
def _rope_tables(S):
    d = DIFF_HEAD_DIM
    inv_freq = ROPE_THETA ** (-np.arange(0, d, 2, dtype=np.float64) / d)
    ang = np.arange(S, dtype=np.float64)[:, None] * inv_freq[None, :]
    ang = np.concatenate([ang, ang, ang, ang], axis=-1)
    return np.asarray(np.cos(ang), np.float32), np.asarray(np.sin(ang), np.float32)


def _split_bf16(a):
    hi = a.astype(BF16)
    lo = (a - hi.astype(F32)).astype(BF16)
    return hi, lo


def _dot(a, b):
    return jnp.dot(a, b, preferred_element_type=F32)


def _dot_split(a, b):
    ah, al = _split_bf16(a)
    bh, bl = _split_bf16(b)
    return _dot(ah, bh) + _dot(al, bh) + _dot(ah, bl)


def _shift_rows(u):
    n = u.shape[0]
    row = lax.broadcasted_iota(jnp.int32, u.shape, 0)
    prev = jnp.where(row == 0, 0.0, pltpu.roll(u, 1, 0))
    nxt = jnp.where(row == n - 1, 0.0, pltpu.roll(u, n - 1, 0))
    return prev, nxt


def _layer_norm(r, g, b):
    mu = jnp.mean(r, axis=-1, keepdims=True)
    rc = r - mu
    var = jnp.mean(rc * rc, axis=-1, keepdims=True)
    return rc * lax.rsqrt(var + LN_EPS) * g + b


def _in_proj_kernel(x_ref, mem_ref, w_ref, wkv_ref, o_ref, xb_ref, kv_ref, *, mq_col):
    d = MEM_HEAD_DIM

    @pl.when(pl.program_id(1) == 0)
    def _():
        kv_ref[...] = _dot(mem_ref[0].astype(BF16), wkv_ref[...]).astype(BF16)

    xb_ref[...] = x_ref[0].astype(BF16)
    mq = _dot(xb_ref[...], w_ref[:, mq_col:]).astype(BF16)
    scores = []
    for h in range(MEM_HEADS):
        k = kv_ref[:, h * d:(h + 1) * d]
        scores.append(lax.dot_general(mq[:, h * d:(h + 1) * d], k, (((1,), (1,)), ((), ())),
                                      preferred_element_type=F32) * (MEM_HEAD_DIM ** -0.5))
    for j in range(mq_col // PROJ_TN):
        c = slice(j * PROJ_TN, (j + 1) * PROJ_TN)
        o_ref[0, :, c] = _dot(xb_ref[...], w_ref[:, c]).astype(o_ref.dtype)
    for h in range(MEM_HEADS):
        s = scores[h]
        p = jnp.exp(s - jnp.max(s, axis=-1, keepdims=True))
        pn = (p * (1.0 / jnp.sum(p, axis=-1, keepdims=True))).astype(BF16)
        v = kv_ref[:, MEM_WIDTH + h * d:MEM_WIDTH + (h + 1) * d]
        o_ref[0, :, mq_col + h * d:mq_col + (h + 1) * d] = _dot(pn, v).astype(o_ref.dtype)


def _in_proj(x, mem, w_bf16, wkv_bf16, mq_col):
    B, S, K = x.shape
    M = mem.shape[1]
    N = w_bf16.shape[1]
    tm = PROJ_TM
    return pl.pallas_call(
        functools.partial(_in_proj_kernel, mq_col=mq_col),
        out_shape=jax.ShapeDtypeStruct((B, S, N), BF16),
        grid=(B, S // tm),
        in_specs=[pl.BlockSpec((1, tm, K), lambda b, i: (b, i, 0)),
                  pl.BlockSpec((1, M, K), lambda b, i: (b, 0, 0)),
                  _resident((K, N), lambda b, i: (0, 0)),
                  _resident(wkv_bf16.shape, lambda b, i: (0, 0))],
        out_specs=pl.BlockSpec((1, tm, N), lambda b, i: (b, i, 0)),
        scratch_shapes=[pltpu.VMEM((tm, K), BF16), pltpu.VMEM((M, 2 * MEM_WIDTH), BF16)],
        compiler_params=_cparams(("parallel", "arbitrary")),
        name="in_proj_mem_attn",
    )(x, mem, w_bf16, wkv_bf16)


def _filter_kernel(z_ref, t_ref, dl_ref, w1_ref, b1_ref, fq_ref, w2_ref, b2_ref, w3_ref,
                   e_ref, d_ref, kn_ref, *, L):
    tl = z_ref.shape[0]
    C = HYENA_WIDTH
    fq = fq_ref[...]
    h = jnp.sin(fq * (_dot_split(z_ref[...], w1_ref[...]) + b1_ref[...]))
    h = jnp.sin(fq * (_dot_split(h, w2_ref[...]) + b2_ref[...]))
    decay = jnp.exp(-t_ref[...] * dl_ref[...])
    row = pl.program_id(0) * tl + lax.broadcasted_iota(jnp.int32, (tl, C), 0)
    sign = jnp.where(row % 2 == 0, 1.0, -1.0)

    @pl.when(pl.program_id(0) == 0)
    def _():
        kn_ref[...] = jnp.zeros_like(kn_ref)

    for o in range(HYENA_ORDER):
        ho = _dot_split(h, w3_ref[:, o * 2 * C:(o + 1) * 2 * C])
        fwd = ho[:, :C] * decay
        bwd = ho[:, C:] * decay
        e = fwd + jnp.where(row == 0, 0.0, bwd)
        d = fwd - bwd
        e_ref[:, o * C:(o + 1) * C] = e.astype(BF16)
        d_ref[:, o * C:(o + 1) * C] = d.astype(BF16)
        kn_ref[:, o * C:(o + 1) * C] += jnp.sum(e * sign, axis=0, keepdims=True) * (1.0 / (2 * L))


def _filter_time(L, w1, b1, freq, w2, b2, w3):
    zp, t, deltas = _filter_features(L)
    H = LANES
    pad_c = H - FILTER_HIDDEN
    w1p = jnp.pad(w1.astype(F32), ((0, LANES - FILTER_EMB), (0, pad_c)))
    w2p = jnp.pad(w2.astype(F32), ((0, pad_c), (0, pad_c)))
    w3p = jnp.pad(w3.astype(F32), ((0, pad_c), (0, 0)))
    row = lambda v: jnp.pad(v.astype(F32), (0, pad_c)).reshape(1, H)
    OC = HYENA_ORDER * HYENA_WIDTH
    outs = [jax.ShapeDtypeStruct((L, OC), BF16)] * 2 + [jax.ShapeDtypeStruct((1, OC), F32)]
    tl = DFT_TK
    rows = lambda w: pl.BlockSpec((tl, w), lambda i: (i, 0))
    whole = lambda r, c: pl.BlockSpec((r, c), lambda i: (0, 0))
    return pl.pallas_call(
        functools.partial(_filter_kernel, L=L),
        out_shape=outs,
        grid=(L // tl,),
        in_specs=[rows(H), rows(1), whole(1, HYENA_WIDTH), whole(H, H), whole(1, H), whole(1, H),
                  whole(H, H), whole(1, H), whole(H, 2 * OC)],
        out_specs=[rows(OC)] * 2 + [whole(1, OC)],
        compiler_params=_cparams(("arbitrary",)),
        name="hyena_filter_time",
    )(jnp.asarray(zp), jnp.asarray(t), jnp.asarray(deltas), w1p, row(b1), row(freq), w2p, row(b2), w3p)


def _spectrum_kernel(ca_ref, sa_ref, cb_ref, sb_ref, e_ref, d_ref, c_ref, s_ref, q1_ref, q2_ref):
    tk, L = c_ref.shape
    cb, sb = cb_ref[...], sb_ref[...]
    for a in range(L // LANES):
        ca = ca_ref[:, a:a + 1]
        sa = sa_ref[:, a:a + 1]
        c_ref[:, a * LANES:(a + 1) * LANES] = (ca * cb - sa * sb).astype(BF16)
        s_ref[:, a * LANES:(a + 1) * LANES] = (sa * cb + ca * sb).astype(BF16)
    kc = _dot(c_ref[...], e_ref[...])
    ks = _dot(s_ref[...], d_ref[...])
    k_glob = pl.program_id(0) * tk + lax.broadcasted_iota(jnp.int32, kc.shape, 0)
    scale = jnp.where(k_glob == 0, 0.5 / L, 1.0 / L)
    q1_ref[...] = kc * scale
    q2_ref[...] = ks * scale


def _spectrum(L, e, d):
    ca, sa, cb, sb = _dft_factor_tables(L)
    OC = e.shape[1]
    tk = DFT_TK
    na = L // LANES
    rows = lambda w: pl.BlockSpec((tk, w), lambda i: (i, 0))
    full = _resident((L, OC), lambda i: (0, 0))
    return pl.pallas_call(
        _spectrum_kernel,
        out_shape=[jax.ShapeDtypeStruct((L, L), BF16), jax.ShapeDtypeStruct((L, L), BF16),
                   jax.ShapeDtypeStruct((L, OC), F32), jax.ShapeDtypeStruct((L, OC), F32)],
        grid=(L // tk,),
        in_specs=[rows(na), rows(na), rows(LANES), rows(LANES), full, full],
        out_specs=[rows(L), rows(L), rows(OC), rows(OC)],
        compiler_params=_cparams(("parallel",)),
        name="hyena_filter_spectrum",
    )(jnp.asarray(ca), jnp.asarray(sa), jnp.asarray(cb), jnp.asarray(sb), e, d)


def _hyena_kernel(v_ref, x1_ref, x2_ref, wv_ref, wx1_ref, wx2_ref, bv_ref, bx1_ref, bx2_ref,
                  c_ref, s_ref, q1a_ref, q2a_ref, q1b_ref, q2b_ref, kna_ref, knb_ref,
                  bias_a_ref, bias_b_ref, o_ref, zb_ref, y_ref, z_ref):
    L = v_ref.shape[1]
    kb_n, mb_n = L // HY_KB, L // HY_MB

    def short_conv(u_ref, w_ref, b_ref):
        u = u_ref[0].astype(F32)
        prev, nxt = _shift_rows(u)
        return prev * w_ref[0:1, :] + u * w_ref[1:2, :] + nxt * w_ref[2:3, :] + b_ref[...]

    def long_conv(q1_ref, q2_ref, kn_ref, bias_ref):
        z = z_ref[...]
        zb_ref[...] = z.astype(BF16)
        row = lax.broadcasted_iota(jnp.int32, z.shape, 0)
        sign = jnp.where(row % 2 == 0, 1.0, -1.0)
        z_nyq = jnp.sum(z * sign, axis=0, keepdims=True)
        y_ref[...] = sign * (z_nyq * kn_ref[...]) + z * bias_ref[...]

        def forward(kb):
            r = slice(kb * HY_KB, (kb + 1) * HY_KB)
            return _dot(c_ref[r, :], zb_ref[...]), _dot(s_ref[r, :], zb_ref[...])

        zc, zs = forward(0)
        for kb in range(kb_n):
            if kb + 1 < kb_n:
                nxt = forward(kb + 1)
            r = slice(kb * HY_KB, (kb + 1) * HY_KB)
            q1, q2 = q1_ref[r, :], q2_ref[r, :]
            ya = (zc * q1 - zs * q2).astype(BF16)
            yb = (zc * q2 + zs * q1).astype(BF16)
            for mb in range(mb_n):
                m = slice(mb * HY_MB, (mb + 1) * HY_MB)
                y_ref[m, :] += _dot(c_ref[m, r], ya) + _dot(s_ref[m, r], yb)
            if kb + 1 < kb_n:
                zc, zs = nxt

    z_ref[...] = short_conv(v_ref, wv_ref, bv_ref)
    long_conv(q1a_ref, q2a_ref, kna_ref, bias_a_ref)
    z_ref[...] = short_conv(x1_ref, wx1_ref, bx1_ref) * y_ref[...]
    long_conv(q1b_ref, q2b_ref, knb_ref, bias_b_ref)
    o_ref[0] = (short_conv(x2_ref, wx2_ref, bx2_ref) * y_ref[...]).astype(o_ref.dtype)


def _hyena(proj3, conv_w, conv_b, ctab, stab, q1, q2, knyq, bias):
    B, L, _ = proj3.shape
    C = HYENA_WIDTH
    ct = HY_CT
    nct = C // ct
    conv_w = conv_w.astype(F32)
    conv_b2 = conv_b.astype(F32).reshape(1, 3 * C)
    bias = bias.astype(F32)
    u_spec = lambda g: pl.BlockSpec((1, L, ct), lambda c, b, g=g: (b, 0, g * nct + c))
    w_spec = lambda g: pl.BlockSpec((3, ct), lambda c, b, g=g: (0, g * nct + c))
    b_spec = lambda g: pl.BlockSpec((1, ct), lambda c, b, g=g: (0, g * nct + c))
    tab = _resident((L, L), lambda c, b: (0, 0))
    q_spec = lambda o: _resident((L, ct), lambda c, b, o=o: (0, o * nct + c))
    r_spec = lambda o: pl.BlockSpec((1, ct), lambda c, b, o=o: (0, o * nct + c))
    bias2 = bias.reshape(1, HYENA_ORDER * C)
    return pl.pallas_call(
        _hyena_kernel,
        out_shape=jax.ShapeDtypeStruct((B, L, C), BF16),
        grid=(nct, B),
        in_specs=[u_spec(0), u_spec(1), u_spec(2), w_spec(0), w_spec(1), w_spec(2),
                  b_spec(0), b_spec(1), b_spec(2), tab, tab,
                  q_spec(0), q_spec(0), q_spec(1), q_spec(1), r_spec(0), r_spec(1), r_spec(0), r_spec(1)],
        out_specs=pl.BlockSpec((1, L, ct), lambda c, b: (b, 0, c)),
        scratch_shapes=[pltpu.VMEM((L, ct), BF16), pltpu.VMEM((L, ct), F32), pltpu.VMEM((L, ct), F32)],
        compiler_params=_cparams(("parallel", "parallel")),
        name="hyena_mixer",
    )(proj3, proj3, proj3, conv_w, conv_w, conv_w, conv_b2, conv_b2, conv_b2, ctab, stab,
      q1, q2, q1, q2, knyq, knyq, bias2, bias2)


def _rope(t, cos, sin):
    lane = lax.broadcasted_iota(jnp.int32, t.shape, 1)
    first = (lane % DIFF_HEAD_DIM) < (DIFF_HEAD_DIM // 2)
    up = pltpu.roll(t, LANES - DIFF_HEAD_DIM // 2, 1)
    down = pltpu.roll(t, DIFF_HEAD_DIM // 2, 1)
    rot = jnp.where(first, -up, down)
    return t * cos + rot * sin


def _diff_attn_kernel(q_ref, k_ref, v_ref, cos_ref, sin_ref, lam_ref, g_ref, o_ref, kr_ref, vt_ref,
                      *, lambda_init):
    S, hw = k_ref.shape[1], k_ref.shape[2]
    tq = DA_TQ
    kr_ref[...] = _rope(k_ref[0].astype(F32), cos_ref[...], sin_ref[...]).astype(BF16)
    vt_ref[0:hw, :] = v_ref[0].astype(F32).T.astype(BF16)
    vt_ref[hw:, :] = jnp.ones((DA_ONES_ROWS, S), BF16)
    lp = lam_ref[...]
    lam = (jnp.exp(jnp.sum(lp[0:1] * lp[1:2], axis=-1, keepdims=True))
           - jnp.exp(jnp.sum(lp[2:3] * lp[3:4], axis=-1, keepdims=True)) + lambda_init)
    scale = DIFF_HEAD_DIM ** -0.5 * LOG2_E

    def scores(r):
        q = (_rope(q_ref[0, r, :].astype(F32), cos_ref[r, :], sin_ref[r, :]) * scale).astype(BF16)
        lane = lax.broadcasted_iota(jnp.int32, q.shape, 1)
        zero = jnp.zeros_like(q)
        qq = jnp.concatenate([jnp.where(lane < DIFF_HEAD_DIM, q, zero),
                              jnp.where(lane >= DIFF_HEAD_DIM, q, zero)], axis=0)
        return lax.dot_general(kr_ref[...], qq, (((1,), (1,)), ((), ())), preferred_element_type=F32)

    def finish(st, r):
        m = jnp.max(st, axis=0, keepdims=True)
        p = jnp.exp2(st - m).astype(BF16)
        ov = _dot(vt_ref[...], p)
        on = ov[:hw, :] * (1.0 / ov[hw:hw + 1, :])
        o = on[:, :tq] - lam * on[:, tq:]
        o = o * lax.rsqrt(jnp.mean(o * o, axis=0, keepdims=True) + RMS_EPS)
        o = o * g_ref[...] * (1.0 - lambda_init)
        o_ref[0, r, :] = o.T.astype(o_ref.dtype)

    blocks = [slice(i * tq, (i + 1) * tq) for i in range(S // tq)]
    s = scores(blocks[0])
    for i, r in enumerate(blocks):
        if i + 1 < len(blocks):
            nxt = scores(blocks[i + 1])
        finish(s, r)
        if i + 1 < len(blocks):
            s = nxt


def _diff_attn(proj3, lam_params, subln_g, lambda_init, q_col, k_col, v_col):
    B, S, _ = proj3.shape
    hw = 2 * DIFF_HEAD_DIM
    cos, sin = _rope_tables(S)
    cos, sin = jnp.asarray(cos), jnp.asarray(sin)
    qb, kb, vb = q_col // hw, k_col // hw, v_col // hw
    head = lambda c0: pl.BlockSpec((1, S, hw), lambda b, h, c0=c0: (b, 0, c0 + h))
    table = pl.BlockSpec((S, hw), lambda b, h: (0, 0))
    return pl.pallas_call(
        functools.partial(_diff_attn_kernel, lambda_init=lambda_init),
        out_shape=jax.ShapeDtypeStruct((B, S, DIFF_WIDTH), BF16),
        grid=(B, DIFF_HEADS),
        in_specs=[head(qb), head(kb), head(vb), table, table,
                  pl.BlockSpec((4, DIFF_HEAD_DIM), lambda b, h: (0, 0)),
                  pl.BlockSpec((hw, 1), lambda b, h: (0, 0))],
        out_specs=pl.BlockSpec((1, S, hw), lambda b, h: (b, 0, h)),
        scratch_shapes=[pltpu.VMEM((S, hw), BF16), pltpu.VMEM((hw + DA_ONES_ROWS, S), BF16)],
        compiler_params=_cparams(("parallel", "parallel")),
        name="diff_attention",
    )(proj3, proj3, proj3, cos, sin, lam_params.astype(F32), subln_g.astype(F32).reshape(hw, 1))


def _out_ln_kernel(yh_ref, yd_ref, ym_ref, x_ref, w_ref, g_ref, b_ref, o_ref, cat_ref):
    wh, wd = yh_ref.shape[1], yd_ref.shape[1]
    cat_ref[:, 0:wh] = yh_ref[...]
    cat_ref[:, wh:wh + wd] = yd_ref[...]
    cat_ref[:, wh + wd:] = ym_ref[...]
    tiles = [slice(i * OUT_SUB, (i + 1) * OUT_SUB) for i in range(cat_ref.shape[0] // OUT_SUB)]
    mix = _dot(cat_ref[tiles[0], :], w_ref[...])
    for i, r in enumerate(tiles):
        if i + 1 < len(tiles):
            nxt = _dot(cat_ref[tiles[i + 1], :], w_ref[...])
        o_ref[r, :] = _layer_norm(DEEPNORM_ALPHA * x_ref[r, :] + mix, g_ref[...], b_ref[...])
        if i + 1 < len(tiles):
            mix = nxt


def _out_ln(yh, yd, proj2d, ym_col, x2d, w_out_bf16, g, b):
    M, D = x2d.shape
    tm = OUT_TM
    rows = lambda w: pl.BlockSpec((tm, w), lambda i: (i, 0))
    vec = pl.BlockSpec((1, D), lambda i: (0, 0))
    ymb = ym_col // MEM_WIDTH
    return pl.pallas_call(
        _out_ln_kernel,
        out_shape=jax.ShapeDtypeStruct((M, D), F32),
        grid=(M // tm,),
        in_specs=[rows(yh.shape[1]), rows(yd.shape[1]),
                  pl.BlockSpec((tm, MEM_WIDTH), lambda i: (i, ymb)), rows(D),
                  _resident(w_out_bf16.shape, lambda i: (0, 0)), vec, vec],
        out_specs=rows(D),
        scratch_shapes=[pltpu.VMEM((tm, w_out_bf16.shape[0]), BF16)],
        compiler_params=_cparams(("parallel",)),
        name="out_proj_ln",
    )(yh, yd, proj2d, x2d, w_out_bf16, g.astype(F32).reshape(1, D), b.astype(F32).reshape(1, D))


def _ffn_kernel(xm_ref, xp_ref, xn_ref, wgu_ref, cw_ref, cb_ref, wdn_ref, g_ref, b_ref, o_ref,
                xe_ref, act_ref):
    tm = xm_ref.shape[1]
    halo = xp_ref.shape[1]
    nchunk = wgu_ref.shape[0]
    cw = wgu_ref.shape[2] // 2
    i = pl.program_id(1)
    last = pl.num_programs(1) - 1
    xe_ref[0:halo, :] = jnp.where(i == 0, 0.0, xp_ref[0]).astype(BF16)
    xe_ref[halo:halo + tm, :] = xm_ref[0].astype(BF16)
    xe_ref[halo + tm:, :] = jnp.where(i == last, 0.0, xn_ref[0]).astype(BF16)

    def up(j):
        return _dot(xe_ref[...], wgu_ref[j])

    def conv(hc, j):
        n = hc.shape[0]
        prev = pltpu.roll(hc, 1, 0)[halo:halo + tm]
        nxt = pltpu.roll(hc, n - 1, 0)[halo:halo + tm]
        w = cw_ref[j]
        return prev * w[0:1] + hc[halo:halo + tm] * w[1:2] + nxt * w[2:3] + cb_ref[j]

    h = up(0)
    for j in range(nchunk):
        if j + 1 < nchunk:
            nxt = up(j + 1)
        c = conv(h, j)
        hg, hu = c[:, :cw], c[:, cw:]
        act = hg * (1.0 / (1.0 + jnp.exp(-hg))) * hu
        act_ref[:, j * cw:(j + 1) * cw] = act.astype(BF16)
        if j + 1 < nchunk:
            h = nxt
    y = _dot(act_ref[...], wdn_ref[...])
    o_ref[0] = _layer_norm(DEEPNORM_ALPHA * xm_ref[0] + y, g_ref[...], b_ref[...])


def _ffn(x3, w_up, conv_w, conv_b, w_down, g, b):
    B, S, D = x3.shape
    tm, cw, halo = FFN_TM, FFN_CW, FFN_HALO
    nchunk = D_FF // cw
    pair = lambda a: jnp.concatenate([a[..., :D_FF].reshape(a.shape[0], nchunk, cw),
                                      a[..., D_FF:].reshape(a.shape[0], nchunk, cw)], axis=-1).transpose(1, 0, 2)
    wgu = pair(w_up.astype(BF16))
    cwt = pair(conv_w.astype(F32))
    cbt = pair(conv_b.astype(F32).reshape(1, -1))
    wdn = w_down.astype(BF16)
    r = tm // halo
    nblk = S // halo
    vec = pl.BlockSpec((1, D), lambda b_, i: (0, 0))
    return pl.pallas_call(
        _ffn_kernel,
        out_shape=jax.ShapeDtypeStruct((B, S, D), F32),
        grid=(B, S // tm),
        in_specs=[pl.BlockSpec((1, tm, D), lambda b_, i: (b_, i, 0)),
                  pl.BlockSpec((1, halo, D), lambda b_, i: (b_, jnp.maximum(i * r - 1, 0), 0)),
                  pl.BlockSpec((1, halo, D), lambda b_, i: (b_, jnp.minimum((i + 1) * r, nblk - 1), 0)),
                  _resident(wgu.shape, lambda b_, i: (0, 0, 0)),
                  _resident(cwt.shape, lambda b_, i: (0, 0, 0)),
                  _resident(cbt.shape, lambda b_, i: (0, 0, 0)),
                  _resident(wdn.shape, lambda b_, i: (0, 0)),
                  vec, vec],
        out_specs=pl.BlockSpec((1, tm, D), lambda b_, i: (b_, i, 0)),
        scratch_shapes=[pltpu.VMEM((tm + 2 * halo, D), BF16), pltpu.VMEM((tm, D_FF), BF16)],
        compiler_params=_cparams(("parallel", "parallel")),
        name="conv_gated_mlp_ln",
    )(x3, x3, x3, wgu, cwt, cbt, wdn, g.astype(F32).reshape(1, D), b.astype(F32).reshape(1, D))


def kernel(x, mem, w_in, hy_conv_w, hy_conv_b, hy_w1, hy_b1, hy_freq, hy_w2, hy_b2, hy_w3, hy_bias,
           diff_lambda, diff_subln_g, mem_w_kv, w_out, ln1_g, ln1_b, ffn_w_up, ffn_conv_w, ffn_conv_b,
           ffn_w_down, ln2_g, ln2_b):
    B, S, D = x.shape
    s1 = 3 * HYENA_WIDTH
    q_col, k_col, v_col = s1, s1 + DIFF_WIDTH, s1 + 2 * DIFF_WIDTH
    mq_col = s1 + 3 * DIFF_WIDTH
    for l in range(DEPTH):
        lambda_init = 0.8 - 0.6 * math.exp(-0.3 * l)
        proj3 = _in_proj(x, mem, w_in[l].astype(BF16), mem_w_kv[l].astype(BF16), mq_col)

        e, d, knyq = _filter_time(S, hy_w1[l], hy_b1[l], hy_freq[l], hy_w2[l], hy_b2[l], hy_w3[l])
        ctab, stab, q1, q2 = _spectrum(S, e, d)
        y_h = _hyena(proj3, hy_conv_w[l], hy_conv_b[l], ctab, stab, q1, q2, knyq, hy_bias[l])
        y_d = _diff_attn(proj3, diff_lambda[l], diff_subln_g[l], lambda_init, q_col, k_col, v_col)

        x1 = _out_ln(y_h.reshape(B * S, -1), y_d.reshape(B * S, -1), proj3.reshape(B * S, -1), mq_col,
                     x.reshape(B * S, D), w_out[l].astype(BF16), ln1_g[l], ln1_b[l])
        x = _ffn(x1.reshape(B, S, D), ffn_w_up[l], ffn_conv_w[l], ffn_conv_b[l], ffn_w_down[l],
                 ln2_g[l], ln2_b[l])
    return x
```

```python
import functools
import math

import jax
import jax.numpy as jnp
import numpy as np
from jax import lax
from jax.experimental import pallas as pl
from jax.experimental.pallas import tpu as pltpu

F32 = jnp.float32
BF16 = jnp.bfloat16

HYENA_WIDTH = 512
HYENA_ORDER = 2
FILTER_EMB = 33
FILTER_HIDDEN = 64
HYENA_TARGET = 1e-2
HYENA_MIN_DECAY = math.log(HYENA_TARGET) / 0.3
HYENA_MAX_DECAY = math.log(HYENA_TARGET) / 1.5
DIFF_HEADS = 4
DIFF_HEAD_DIM = 64
DIFF_WIDTH = DIFF_HEADS * 2 * DIFF_HEAD_DIM
MEM_HEADS = 4
MEM_HEAD_DIM = 128
MEM_WIDTH = MEM_HEADS * MEM_HEAD_DIM
D_FF = 2816
ROPE_THETA = 10000.0
LN_EPS = 1e-5
RMS_EPS = 1e-5
DEPTH = 1
DEEPNORM_ALPHA = (2.0 * DEPTH) ** 0.25
LOG2_E = 1.4426950408889634

LANES = 128
SUBLANES = 8
VMEM_LIMIT_BYTES = 56 * 1024 * 1024

PROJ_TM = 1024
PROJ_TN = 512
HY_CT = 256
HY_KB = 512
HY_MB = 512
DFT_TK = 256
DA_TQ = 512
DA_ONES_ROWS = 2 * SUBLANES
OUT_TM = 1024
OUT_SUB = 512
FFN_TM = 512
FFN_CW = 256
FFN_HALO = SUBLANES


def _cparams(sem):
    return pltpu.CompilerParams(dimension_semantics=sem, vmem_limit_bytes=VMEM_LIMIT_BYTES)


def _resident(block_shape, index_map):
    return pl.BlockSpec(block_shape, index_map, pipeline_mode=pl.Buffered(1))


def _angle_factors(rows, col_a, col_b, L):
    rows = np.asarray(rows, np.int64)[:, None]
    ang_a = np.pi * ((rows * np.asarray(col_a, np.int64)[None, :]) % (2 * L)).astype(np.float64) / L
    ang_b = np.pi * ((rows * np.asarray(col_b, np.int64)[None, :]) % (2 * L)).astype(np.float64) / L
    f = lambda v: np.asarray(v, dtype=np.float32)
    return f(np.cos(ang_a)), f(np.sin(ang_a)), f(np.cos(ang_b)), f(np.sin(ang_b))


@functools.lru_cache(maxsize=None)
def _spectrum_factor_tables(L):
    H = L // 2
    freqs = np.concatenate([np.arange(H), L - np.arange(H)])
    ca, sa, cb, sb = _angle_factors(freqs, LANES * np.arange(L // LANES), np.arange(LANES), L)
    scale = np.where((freqs == 0) | (freqs == L), 0.5 / L, 1.0 / L).astype(np.float32)[:, None]
    return ca, sa, cb, sb, scale


@functools.lru_cache(maxsize=None)
def _half_dft_factor_tables(L):
    H = L // 2
    idx = np.arange(H)
    a_idx, b_idx = np.arange(H // LANES), np.arange(LANES)
    parts = [_angle_factors(idx, 2 * LANES * a_idx, 2 * b_idx, L),
             _angle_factors(idx, 2 * LANES * a_idx, 2 * b_idx + 1, L),
             _angle_factors(2 * idx + 1, LANES * a_idx, b_idx, L)]
    return tuple(np.concatenate([p[i] for p in parts], axis=0) for i in range(4))


@functools.lru_cache(maxsize=None)
def _filter_features(L):
    t = np.linspace(0.0, 1.0, L, dtype=np.float64)[:, None]
    bands = (FILTER_EMB - 1) // 2
    fr = np.linspace(1e-4, bands - 1, bands, dtype=np.float64)[None, :]
    w = 2.0 * np.pi * np.arange(L, dtype=np.float64)[:, None] / L
    z = np.concatenate([t, np.cos(fr * w), -np.sin(fr * w)], axis=-1)
    zp = np.zeros((L, LANES), np.float64)
    zp[:, :FILTER_EMB] = z
    deltas = np.abs(np.linspace(HYENA_MIN_DECAY, HYENA_MAX_DECAY, HYENA_WIDTH, dtype=np.float64))[None, :]
    return np.asarray(zp, np.float32), np.asarray(t, np.float32), np.asarray(deltas, np.float32)


@functools.lru_cache(maxsize=None)
def _rope_tables(S):
    d = DIFF_HEAD_DIM
    inv_freq = ROPE_THETA ** (-np.arange(0, d, 2, dtype=np.float64) / d)
    ang = np.arange(S, dtype=np.float64)[:, None] * inv_freq[None, :]
    ang = np.concatenate([ang, ang, ang, ang], axis=-1)
    return np.asarray(np.cos(ang), np.float32), np.asarray(np.sin(ang), np.float32)


def _split_bf16(a):
    hi = a.astype(BF16)
    lo = (a - hi.astype(F32)).astype(BF16)
    return hi, lo


def _dot(a, b):
    return jnp.dot(a, b, preferred_element_type=F32)


def _dot_split(a, b):
    ah, al = _split_bf16(a)
    bh, bl = _split_bf16(b)
    return _dot(ah, bh) + _dot(al, bh) + _dot(ah, bl)


def _layer_norm(r, g, b):
    mu = jnp.mean(r, axis=-1, keepdims=True)
    rc = r - mu
    var = jnp.mean(rc * rc, axis=-1, keepdims=True)
    return rc * lax.rsqrt(var + LN_EPS) * g + b


def _in_proj_kernel(x_ref, mem_ref, w_ref, wkv_ref, o_ref, xb_ref, kv_ref, *, mq_col):
    d = MEM_HEAD_DIM

    @pl.when(pl.program_id(1) == 0)
    def _():
        kv_ref[...] = _dot(mem_ref[0].astype(BF16), wkv_ref[...]).astype(BF16)

    xb_ref[...] = x_ref[0].astype(BF16)
    mq = _dot(xb_ref[...], w_ref[:, mq_col:]).astype(BF16)
    scores = []
    for h in range(MEM_HEADS):
        k = kv_ref[:, h * d:(h + 1) * d]
        scores.append(lax.dot_general(mq[:, h * d:(h + 1) * d], k, (((1,), (1,)), ((), ())),
                                      preferred_element_type=F32) * (MEM_HEAD_DIM ** -0.5))
    for j in range(mq_col // PROJ_TN):
        c = slice(j * PROJ_TN, (j + 1) * PROJ_TN)
        o_ref[0, :, c] = _dot(xb_ref[...], w_ref[:, c]).astype(o_ref.dtype)
    for h in range(MEM_HEADS):
        s = scores[h]
        p = jnp.exp(s - jnp.max(s, axis=-1, keepdims=True))
        pn = (p * (1.0 / jnp.sum(p, axis=-1, keepdims=True))).astype(BF16)
        v = kv_ref[:, MEM_WIDTH + h * d:MEM_WIDTH + (h + 1) * d]
        o_ref[0, :, mq_col + h * d:mq_col + (h + 1) * d] = _dot(pn, v).astype(o_ref.dtype)


def _in_proj(x, mem, w_bf16, wkv_bf16, mq_col):
    B, S, K = x.shape
    M = mem.shape[1]
    N = w_bf16.shape[1]
    tm = PROJ_TM
    return pl.pallas_call(
        functools.partial(_in_proj_kernel, mq_col=mq_col),
        out_shape=jax.ShapeDtypeStruct((B, S, N), BF16),
        grid=(B, S // tm),
        in_specs=[pl.BlockSpec((1, tm, K), lambda b, i: (b, i, 0)),
                  pl.BlockSpec((1, M, K), lambda b, i: (b, 0, 0)),
                  _resident((K, N), lambda b, i: (0, 0)),
                  _resident(wkv_bf16.shape, lambda b, i: (0, 0))],
        out_specs=pl.BlockSpec((1, tm, N), lambda b, i: (b, i, 0)),
        scratch_shapes=[pltpu.VMEM((tm, K), BF16), pltpu.VMEM((M, 2 * MEM_WIDTH), BF16)],
        compiler_params=_cparams(("parallel", "arbitrary")),
        name="in_proj_mem_attn",
    )(x, mem, w_bf16, wkv_bf16)


def _filter_kernel(z_ref, t_ref, dl_ref, w1_ref, b1_ref, fq_ref, w2_ref, b2_ref, w3_ref,
                   e_ref, d_ref, km_ref, *, L):
    tl = z_ref.shape[0]
    C = HYENA_WIDTH
    fq = fq_ref[...]
    h = jnp.sin(fq * (_dot_split(z_ref[...], w1_ref[...]) + b1_ref[...]))
    h = jnp.sin(fq * (_dot_split(h, w2_ref[...]) + b2_ref[...]))
    decay = jnp.exp(-t_ref[...] * dl_ref[...])
    row = pl.program_id(0) * tl + lax.broadcasted_iota(jnp.int32, (tl, C), 0)
    cos_mid = jnp.where(row % 4 == 0, 1.0, jnp.where(row % 4 == 2, -1.0, 0.0))
    sin_mid = jnp.where(row % 4 == 1, 1.0, jnp.where(row % 4 == 3, -1.0, 0.0))

    @pl.when(pl.program_id(0) == 0)
    def _():
        km_ref[...] = jnp.zeros_like(km_ref)

    for o in range(HYENA_ORDER):
        ho = _dot_split(h, w3_ref[:, o * 2 * C:(o + 1) * 2 * C])
        fwd = ho[:, :C] * decay
        bwd = ho[:, C:] * decay
        e = fwd + jnp.where(row == 0, 0.0, bwd)
        d = fwd - bwd
        e_ref[:, o * C:(o + 1) * C] = e.astype(BF16)
        d_ref[:, o * C:(o + 1) * C] = d.astype(BF16)
        km_ref[0:1, o * C:(o + 1) * C] += jnp.sum(e * cos_mid, axis=0, keepdims=True) * (1.0 / L)
        km_ref[1:2, o * C:(o + 1) * C] += jnp.sum(d * sin_mid, axis=0, keepdims=True) * (1.0 / L)


def _filter_time(L, w1, b1, freq, w2, b2, w3):
    zp, t, deltas = _filter_features(L)
    H = LANES
    pad_c = H - FILTER_HIDDEN
    w1p = jnp.pad(w1.astype(F32), ((0, LANES - FILTER_EMB), (0, pad_c)))
    w2p = jnp.pad(w2.astype(F32), ((0, pad_c), (0, pad_c)))
    w3p = jnp.pad(w3.astype(F32), ((0, pad_c), (0, 0)))
    row = lambda v: jnp.pad(v.astype(F32), (0, pad_c)).reshape(1, H)
    OC = HYENA_ORDER * HYENA_WIDTH
    outs = [jax.ShapeDtypeStruct((L, OC), BF16)] * 2 + [jax.ShapeDtypeStruct((2, OC), F32)]
    tl = DFT_TK
    rows = lambda w: pl.BlockSpec((tl, w), lambda i: (i, 0))
    whole = lambda r, c: pl.BlockSpec((r, c), lambda i: (0, 0))
    return pl.pallas_call(
        functools.partial(_filter_kernel, L=L),
        out_shape=outs,
        grid=(L // tl,),
        in_specs=[rows(H), rows(1), whole(1, HYENA_WIDTH), whole(H, H), whole(1, H), whole(1, H),
                  whole(H, H), whole(1, H), whole(H, 2 * OC)],
        out_specs=[rows(OC)] * 2 + [whole(2, OC)],
        compiler_params=_cparams(("arbitrary",)),
        name="hyena_filter_time",
    )(jnp.asarray(zp), jnp.asarray(t), jnp.asarray(deltas), w1p, row(b1), row(freq), w2p, row(b2), w3p)


def _fill_tables(ca_ref, sa_ref, cb_ref, sb_ref, c_ref, s_ref):
    cb, sb = cb_ref[...], sb_ref[...]
    for a in range(c_ref.shape[1] // LANES):
        ca = ca_ref[:, a:a + 1]
        sa = sa_ref[:, a:a + 1]
        c_ref[:, a * LANES:(a + 1) * LANES] = (ca * cb - sa * sb).astype(c_ref.dtype)
        s_ref[:, a * LANES:(a + 1) * LANES] = (sa * cb + ca * sb).astype(s_ref.dtype)


def _spectrum_kernel(ca_ref, sa_ref, cb_ref, sb_ref, scale_ref, e_ref, d_ref, q1_ref, q2_ref, c_ref, s_ref):
    _fill_tables(ca_ref, sa_ref, cb_ref, sb_ref, c_ref, s_ref)
    q1_ref[...] = _dot(c_ref[...], e_ref[...]) * scale_ref[...]
    q2_ref[...] = _dot(s_ref[...], d_ref[...]) * scale_ref[...]


def _spectrum(L, e, d):
    ca, sa, cb, sb, scale = _spectrum_factor_tables(L)
    OC = e.shape[1]
    tk = DFT_TK
    rows = lambda w: pl.BlockSpec((tk, w), lambda i: (i, 0))
    full = _resident((L, OC), lambda i: (0, 0))
    return pl.pallas_call(
        _spectrum_kernel,
        out_shape=[jax.ShapeDtypeStruct((L, OC), F32), jax.ShapeDtypeStruct((L, OC), F32)],
        grid=(L // tk,),
        in_specs=[rows(ca.shape[1]), rows(ca.shape[1]), rows(LANES), rows(LANES), rows(1), full, full],
        out_specs=[rows(OC), rows(OC)],
        scratch_shapes=[pltpu.VMEM((tk, L), BF16), pltpu.VMEM((tk, L), BF16)],
        compiler_params=_cparams(("parallel",)),
        name="hyena_filter_spectrum",
    )(jnp.asarray(ca), jnp.asarray(sa), jnp.asarray(cb), jnp.asarray(sb), jnp.asarray(scale), e, d)


def _half_dft_tables(L):
    ca, sa, cb, sb = _half_dft_factor_tables(L)
    n, H = ca.shape[0], L // 2
    tk = DFT_TK
    rows = lambda w: pl.BlockSpec((tk, w), lambda i: (i, 0))
    return pl.pallas_call(
        _fill_tables,
        out_shape=[jax.ShapeDtypeStruct((n, H), BF16), jax.ShapeDtypeStruct((n, H), BF16)],
        grid=(n // tk,),
        in_specs=[rows(ca.shape[1]), rows(ca.shape[1]), rows(LANES), rows(LANES)],
        out_specs=[rows(H), rows(H)],
        compiler_params=_cparams(("parallel",)),
        name="hyena_dft_tables",
    )(jnp.asarray(ca), jnp.asarray(sa), jnp.asarray(cb), jnp.asarray(sb))


def _hyena_kernel(v_ref, x1_ref, x2_ref, wv_ref, wx1_ref, wx2_ref, bv_ref, bx1_ref, bx2_ref,
                  c_ref, s_ref, q1a_ref, q2a_ref, q1b_ref, q2b_ref, kma_ref, kmb_ref,
                  bias_a_ref, bias_b_ref, o_ref,
                  uf_ref, ze_ref, zo_ref, zeb_ref, zob_ref, ye_ref, yo_ref, ge_ref, go_ref):
    L, ct = v_ref.shape[1], v_ref.shape[2]
    H = L // 2
    nl = ct // LANES
    even, odd = pl.ds(0, H, stride=2), pl.ds(1, H, stride=2)
    ce, co, cot = (c_ref.at[pl.ds(i * H, H), :] for i in range(3))
    se, so, sot = (s_ref.at[pl.ds(i * H, H), :] for i in range(3))
    row = lax.broadcasted_iota(jnp.int32, (H, ct), 0)
    alt = jnp.where(row % 2 == 0, 1.0, -1.0)

    def short_conv(u_ref, w_ref, b_ref, oe_ref, oo_ref, gate=None):
        for c in range(nl):
            cols = slice(c * LANES, (c + 1) * LANES)
            uf_ref[c] = u_ref[0, :, cols].astype(F32)
            ue, uo = uf_ref[c, even, :], uf_ref[c, odd, :]
            r = lax.broadcasted_iota(jnp.int32, ue.shape, 0)
            uo_prev = jnp.where(r == 0, 0.0, pltpu.roll(uo, 1, 0))
            ue_next = jnp.where(r == H - 1, 0.0, pltpu.roll(ue, H - 1, 0))
            w0, w1, w2, b = w_ref[0:1, cols], w_ref[1:2, cols], w_ref[2:3, cols], b_ref[:, cols]
            e = uo_prev * w0 + ue * w1 + uo * w2 + b
            o = ue * w0 + uo * w1 + ue_next * w2 + b
            if gate is not None:
                e, o = e * gate[0][:, cols], o * gate[1][:, cols]
            oe_ref[:, cols] = e
            oo_ref[:, cols] = o

    def long_conv(q1_ref, q2_ref, km_ref, bias_ref):
        ze, zo = ze_ref[...], zo_ref[...]
        zeb_ref[...] = ze.astype(BF16)
        zob_ref[...] = zo.astype(BF16)
        zc_mid = jnp.sum(ze * alt, axis=0, keepdims=True)
        zs_mid = jnp.sum(zo * alt, axis=0, keepdims=True)
        kc_mid, ks_mid = km_ref[0:1, :], km_ref[1:2, :]
        bias = bias_ref[...]
        ye_ref[...] = alt * (zc_mid * kc_mid - zs_mid * ks_mid) + ze * bias
        yo_ref[...] = alt * (zc_mid * ks_mid + zs_mid * kc_mid) + zo * bias

        def forward(kb):
            r = pl.ds(kb * HY_KB, HY_KB)
            return (_dot(ce[r, :], zeb_ref[...]), _dot(co[r, :], zob_ref[...]),
                    _dot(se[r, :], zeb_ref[...]), _dot(so[r, :], zob_ref[...]))

        nb = H // HY_KB
        cur = forward(0)
        for kb in range(nb):
            if kb + 1 < nb:
                nxt = forward(kb + 1)
            ec, oc, es, os_ = cur
            lo = slice(kb * HY_KB, (kb + 1) * HY_KB)
            hi = slice(H + kb * HY_KB, H + (kb + 1) * HY_KB)
            zc1, zs1 = ec + oc, es + os_
            zc2, zs2 = ec - oc, os_ - es
            ya1 = zc1 * q1_ref[lo, :] - zs1 * q2_ref[lo, :]
            yb1 = zc1 * q2_ref[lo, :] + zs1 * q1_ref[lo, :]
            ya2 = zc2 * q1_ref[hi, :] - zs2 * q2_ref[hi, :]
            yb2 = zc2 * q2_ref[hi, :] + zs2 * q1_ref[hi, :]
            pe_a, pe_b = (ya1 + ya2).astype(BF16), (yb1 - yb2).astype(BF16)
            po_a, po_b = (ya1 - ya2).astype(BF16), (yb1 + yb2).astype(BF16)
            for mb in range(H // HY_MB):
                m = slice(mb * HY_MB, (mb + 1) * HY_MB)
                ye_ref[m, :] += _dot(ce[m, lo], pe_a) + _dot(se[m, lo], pe_b)
                yo_ref[m, :] += _dot(cot[m, lo], po_a) + _dot(sot[m, lo], po_b)
            if kb + 1 < nb:
                cur = nxt

    short_conv(v_ref, wv_ref, bv_ref, ze_ref, zo_ref)
    long_conv(q1a_ref, q2a_ref, kma_ref, bias_a_ref)
    short_conv(x1_ref, wx1_ref, bx1_ref, ze_ref, zo_ref, gate=(ye_ref, yo_ref))
    long_conv(q1b_ref, q2b_ref, kmb_ref, bias_b_ref)
    short_conv(x2_ref, wx2_ref, bx2_ref, ge_ref, go_ref, gate=(ye_ref, yo_ref))
    for c in range(nl):
        cols = slice(c * LANES, (c + 1) * LANES)
        uf_ref[c, even, :] = ge_ref[:, cols]
        uf_ref[c, odd, :] = go_ref[:, cols]
        o_ref[0, :, cols] = uf_ref[c].astype(o_ref.dtype)


def _hyena(proj3, conv_w, conv_b, ctab, stab, q1, q2, kmid, bias):
    B, L, _ = proj3.shape
    C = HYENA_WIDTH
    ct = HY_CT
    nct = C // ct
    H = L // 2
    conv_w = conv_w.astype(F32)
    conv_b2 = conv_b.astype(F32).reshape(1, 3 * C)
    bias2 = bias.astype(F32).reshape(1, HYENA_ORDER * C)
    u_spec = lambda g: pl.BlockSpec((1, L, ct), lambda c, b, g=g: (b, 0, g * nct + c))
    w_spec = lambda g: pl.BlockSpec((3, ct), lambda c, b, g=g: (0, g * nct + c))
    b_spec = lambda g: pl.BlockSpec((1, ct), lambda c, b, g=g: (0, g * nct + c))
    tab = _resident(ctab.shape, lambda c, b: (0, 0))
    q_spec = lambda o: _resident((L, ct), lambda c, b, o=o: (0, o * nct + c))
    r_spec = lambda rows, o: pl.BlockSpec((rows, ct), lambda c, b, o=o: (0, o * nct + c))
    half_f32 = pltpu.VMEM((H, ct), F32)
    half_bf16 = pltpu.VMEM((H, ct), BF16)
    return pl.pallas_call(
        _hyena_kernel,
        out_shape=jax.ShapeDtypeStruct((B, L, C), BF16),
        grid=(nct, B),
        in_specs=[u_spec(0), u_spec(1), u_spec(2), w_spec(0), w_spec(1), w_spec(2),
                  b_spec(0), b_spec(1), b_spec(2), tab, tab,
                  q_spec(0), q_spec(0), q_spec(1), q_spec(1), r_spec(2, 0), r_spec(2, 1),
                  r_spec(1, 0), r_spec(1, 1)],
        out_specs=pl.BlockSpec((1, L, ct), lambda c, b: (b, 0, c)),
        scratch_shapes=[pltpu.VMEM((ct // LANES, L, LANES), F32), half_f32, half_f32, half_bf16, half_bf16,
                        half_f32, half_f32, half_f32, half_f32],
        compiler_params=_cparams(("parallel", "parallel")),
        name="hyena_mixer",
    )(proj3, proj3, proj3, conv_w, conv_w, conv_w, conv_b2, conv_b2, conv_b2, ctab, stab,
      q1, q2, q1, q2, kmid, kmid, bias2, bias2)


def _rope(t, cos, sin):
    lane = lax.broadcasted_iota(jnp.int32, t.shape, 1)
    first = (lane % DIFF_HEAD_DIM) < (DIFF_HEAD_DIM // 2)
    up = pltpu.roll(t, LANES - DIFF_HEAD_DIM // 2, 1)
    down = pltpu.roll(t, DIFF_HEAD_DIM // 2, 1)
    rot = jnp.where(first, -up, down)
    return t * cos + rot * sin


def _diff_attn_kernel(q_ref, k_ref, v_ref, cos_ref, sin_ref, lam_ref, g_ref, o_ref, kr_ref, vt_ref,
                      *, lambda_init):
    S, hw = k_ref.shape[1], k_ref.shape[2]
    tq = DA_TQ
    kr_ref[...] = _rope(k_ref[0].astype(F32), cos_ref[...], sin_ref[...]).astype(BF16)
    vt_ref[0:hw, :] = v_ref[0].astype(F32).T.astype(BF16)
    vt_ref[hw:, :] = jnp.ones((DA_ONES_ROWS, S), BF16)
    lp = lam_ref[...]
    lam = (jnp.exp(jnp.sum(lp[0:1] * lp[1:2], axis=-1, keepdims=True))
           - jnp.exp(jnp.sum(lp[2:3] * lp[3:4], axis=-1, keepdims=True)) + lambda_init)
    scale = DIFF_HEAD_DIM ** -0.5 * LOG2_E

    def scores(r):
        q = (_rope(q_ref[0, r, :].astype(F32), cos_ref[r, :], sin_ref[r, :]) * scale).astype(BF16)
        lane = lax.broadcasted_iota(jnp.int32, q.shape, 1)
        zero = jnp.zeros_like(q)
        qq = jnp.concatenate([jnp.where(lane < DIFF_HEAD_DIM, q, zero),
                              jnp.where(lane >= DIFF_HEAD_DIM, q, zero)], axis=0)
        return lax.dot_general(kr_ref[...], qq, (((1,), (1,)), ((), ())), preferred_element_type=F32)

    def finish(st, r):
        m = jnp.max(st, axis=0, keepdims=True)
        p = jnp.exp2(st - m).astype(BF16)
        ov = _dot(vt_ref[...], p)
        on = ov[:hw, :] * (1.0 / ov[hw:hw + 1, :])
        o = on[:, :tq] - lam * on[:, tq:]
        o = o * lax.rsqrt(jnp.mean(o * o, axis=0, keepdims=True) + RMS_EPS)
        o = o * g_ref[...] * (1.0 - lambda_init)
        o_ref[0, r, :] = o.T.astype(o_ref.dtype)

    blocks = [slice(i * tq, (i + 1) * tq) for i in range(S // tq)]
    s = scores(blocks[0])
    for i, r in enumerate(blocks):
        if i + 1 < len(blocks):
            nxt = scores(blocks[i + 1])
        finish(s, r)
        if i + 1 < len(blocks):
            s = nxt


def _diff_attn(proj3, lam_params, subln_g, lambda_init, q_col, k_col, v_col):
    B, S, _ = proj3.shape
    hw = 2 * DIFF_HEAD_DIM
    cos, sin = _rope_tables(S)
    cos, sin = jnp.asarray(cos), jnp.asarray(sin)
    qb, kb, vb = q_col // hw, k_col // hw, v_col // hw
    head = lambda c0: pl.BlockSpec((1, S, hw), lambda b, h, c0=c0: (b, 0, c0 + h))
    table = pl.BlockSpec((S, hw), lambda b, h: (0, 0))
    return pl.pallas_call(
        functools.partial(_diff_attn_kernel, lambda_init=lambda_init),
        out_shape=jax.ShapeDtypeStruct((B, S, DIFF_WIDTH), BF16),
        grid=(B, DIFF_HEADS),
        in_specs=[head(qb), head(kb), head(vb), table, table,
                  pl.BlockSpec((4, DIFF_HEAD_DIM), lambda b, h: (0, 0)),
                  pl.BlockSpec((hw, 1), lambda b, h: (0, 0))],
        out_specs=pl.BlockSpec((1, S, hw), lambda b, h: (b, 0, h)),
        scratch_shapes=[pltpu.VMEM((S, hw), BF16), pltpu.VMEM((hw + DA_ONES_ROWS, S), BF16)],
        compiler_params=_cparams(("parallel", "parallel")),
        name="diff_attention",
    )(proj3, proj3, proj3, cos, sin, lam_params.astype(F32), subln_g.astype(F32).reshape(hw, 1))


def _out_ln_kernel(yh_ref, yd_ref, ym_ref, x_ref, w_ref, g_ref, b_ref, o_ref, cat_ref):
    wh, wd = yh_ref.shape[1], yd_ref.shape[1]
    cat_ref[:, 0:wh] = yh_ref[...]
    cat_ref[:, wh:wh + wd] = yd_ref[...]
    cat_ref[:, wh + wd:] = ym_ref[...]
    tiles = [slice(i * OUT_SUB, (i + 1) * OUT_SUB) for i in range(cat_ref.shape[0] // OUT_SUB)]
    mix = _dot(cat_ref[tiles[0], :], w_ref[...])
    for i, r in enumerate(tiles):
        if i + 1 < len(tiles):
            nxt = _dot(cat_ref[tiles[i + 1], :], w_ref[...])
        o_ref[r, :] = _layer_norm(DEEPNORM_ALPHA * x_ref[r, :] + mix, g_ref[...], b_ref[...])
        if i + 1 < len(tiles):
            mix = nxt


def _out_ln(yh, yd, proj2d, ym_col, x2d, w_out_bf16, g, b):
    M, D = x2d.shape
    tm = OUT_TM
    rows = lambda w: pl.BlockSpec((tm, w), lambda i: (i, 0))
    vec = pl.BlockSpec((1, D), lambda i: (0, 0))
    ymb = ym_col // MEM_WIDTH
    return pl.pallas_call(
        _out_ln_kernel,
        out_shape=jax.ShapeDtypeStruct((M, D), F32),
        grid=(M // tm,),
        in_specs=[rows(yh.shape[1]), rows(yd.shape[1]),
                  pl.BlockSpec((tm, MEM_WIDTH), lambda i: (i, ymb)), rows(D),
                  _resident(w_out_bf16.shape, lambda i: (0, 0)), vec, vec],
        out_specs=rows(D),
        scratch_shapes=[pltpu.VMEM((tm, w_out_bf16.shape[0]), BF16)],
        compiler_params=_cparams(("parallel",)),
        name="out_proj_ln",
    )(yh, yd, proj2d, x2d, w_out_bf16, g.astype(F32).reshape(1, D), b.astype(F32).reshape(1, D))


def _ffn_kernel(xm_ref, xp_ref, xn_ref, wup_ref, cw_ref, cb_ref, wdn_ref, g_ref, b_ref, o_ref,
                xe_ref, act_ref):
    tm = xm_ref.shape[1]
    halo = xp_ref.shape[1]
    cw = FFN_CW
    nchunk = D_FF // cw
    i = pl.program_id(1)
    last = pl.num_programs(1) - 1
    xe_ref[0:halo, :] = jnp.where(i == 0, 0.0, xp_ref[0]).astype(BF16)
    xe_ref[halo:halo + tm, :] = xm_ref[0].astype(BF16)
    xe_ref[halo + tm:, :] = jnp.where(i == last, 0.0, xn_ref[0]).astype(BF16)

    def cols(j):
        return slice(j * cw, (j + 1) * cw), slice(D_FF + j * cw, D_FF + (j + 1) * cw)

    def up(j):
        cg, cu = cols(j)
        return _dot(xe_ref[...], wup_ref[:, cg]), _dot(xe_ref[...], wup_ref[:, cu])

    def conv(hc, c):
        n = hc.shape[0]
        prev = pltpu.roll(hc, 1, 0)[halo:halo + tm]
        nxt = pltpu.roll(hc, n - 1, 0)[halo:halo + tm]
        return (prev * cw_ref[0:1, c] + hc[halo:halo + tm] * cw_ref[1:2, c] + nxt * cw_ref[2:3, c]
                + cb_ref[:, c])

    h = up(0)
    for j in range(nchunk):
        if j + 1 < nchunk:
            nxt = up(j + 1)
        cg, cu = cols(j)
        hg = conv(h[0], cg)
        hu = conv(h[1], cu)
        act = hg * (1.0 / (1.0 + jnp.exp(-hg))) * hu
        act_ref[:, j * cw:(j + 1) * cw] = act.astype(BF16)
        if j + 1 < nchunk:
            h = nxt
    y = _dot(act_ref[...], wdn_ref[...])
    o_ref[0] = _layer_norm(DEEPNORM_ALPHA * xm_ref[0] + y, g_ref[...], b_ref[...])


def _ffn(x3, w_up, conv_w, conv_b, w_down, g, b):
    B, S, D = x3.shape
    tm, halo = FFN_TM, FFN_HALO
    wup = w_up.astype(BF16)
    cwt = conv_w.astype(F32)
    cbt = conv_b.astype(F32).reshape(1, -1)
    wdn = w_down.astype(BF16)
    r = tm // halo
    nblk = S // halo
    vec = pl.BlockSpec((1, D), lambda b_, i: (0, 0))
    return pl.pallas_call(
        _ffn_kernel,
        out_shape=jax.ShapeDtypeStruct((B, S, D), F32),
        grid=(B, S // tm),
        in_specs=[pl.BlockSpec((1, tm, D), lambda b_, i: (b_, i, 0)),
                  pl.BlockSpec((1, halo, D), lambda b_, i: (b_, jnp.maximum(i * r - 1, 0), 0)),
                  pl.BlockSpec((1, halo, D), lambda b_, i: (b_, jnp.minimum((i + 1) * r, nblk - 1), 0)),
                  _resident(wup.shape, lambda b_, i: (0, 0)),
                  _resident(cwt.shape, lambda b_, i: (0, 0)),
                  _resident(cbt.shape, lambda b_, i: (0, 0)),
                  _resident(wdn.shape, lambda b_, i: (0, 0)),
                  vec, vec],
        out_specs=pl.BlockSpec((1, tm, D), lambda b_, i: (b_, i, 0)),
        scratch_shapes=[pltpu.VMEM((tm + 2 * halo, D), BF16), pltpu.VMEM((tm, D_FF), BF16)],
        compiler_params=_cparams(("parallel", "parallel")),
        name="conv_gated_mlp_ln",
    )(x3, x3, x3, wup, cwt, cbt, wdn, g.astype(F32).reshape(1, D), b.astype(F32).reshape(1, D))


def kernel(x, mem, w_in, hy_conv_w, hy_conv_b, hy_w1, hy_b1, hy_freq, hy_w2, hy_b2, hy_w3, hy_bias,
           diff_lambda, diff_subln_g, mem_w_kv, w_out, ln1_g, ln1_b, ffn_w_up, ffn_conv_w, ffn_conv_b,
           ffn_w_down, ln2_g, ln2_b):
    B, S, D = x.shape
    s1 = 3 * HYENA_WIDTH
    q_col, k_col, v_col = s1, s1 + DIFF_WIDTH, s1 + 2 * DIFF_WIDTH
    mq_col = s1 + 3 * DIFF_WIDTH
    for l in range(DEPTH):
        lambda_init = 0.8 - 0.6 * math.exp(-0.3 * l)
        proj3 = _in_proj(x, mem, w_in[l].astype(BF16), mem_w_kv[l].astype(BF16), mq_col)

        e, d, kmid = _filter_time(S, hy_w1[l], hy_b1[l], hy_freq[l], hy_w2[l], hy_b2[l], hy_w3[l])
        q1, q2 = _spectrum(S, e, d)
        ctab, stab = _half_dft_tables(S)
        y_h = _hyena(proj3, hy_conv_w[l], hy_conv_b[l], ctab, stab, q1, q2, kmid, hy_bias[l])
        y_d = _diff_attn(proj3, diff_lambda[l], diff_subln_g[l], lambda_init, q_col, k_col, v_col)

        x1 = _out_ln(y_h.reshape(B * S, -1), y_d.reshape(B * S, -1), proj3.reshape(B * S, -1), mq_col,
                     x.reshape(B * S, D), w_out[l].astype(BF16), ln1_g[l], ln1_b[l])
        x = _ffn(x1.reshape(B, S, D), ffn_w_up[l], ffn_conv_w[l], ffn_conv_b[l], ffn_w_down[l],
                 ln2_g[l], ln2_b[l])
    return x
```
